```python
import jax, jax.numpy as jnp
from jax import lax
import numpy as np

D_MODEL = 2048
BATCH = 2
SEQ = 8192
DEPTH = 4

N_MIXERS = 2
N_CONV_LAYERS = (DEPTH + 1) // 2
N_GLA_LAYERS = DEPTH // 2
CONV_WIDTH = 31
GLA_HEADS = 4
GLA_DK = D_MODEL // 2
GLA_DV = D_MODEL
GLA_HEAD_K = GLA_DK // GLA_HEADS
GLA_HEAD_V = GLA_DV // GLA_HEADS
GLA_GATE_RANK = 16
GLA_GATE_TAU = 16.0
GLA_CHUNK = 64
GLA_IN_WIDTH = 2 * GLA_DK + 2 * GLA_DV + GLA_GATE_RANK
D_FF = -(-8 * D_MODEL // (3 * 256)) * 256
N_MOD = 6
EPS = 1e-6

kernel_name = "hybrid_conformerconv_gla_sandwich_adaln"


def rms_norm(x, gain):
    xf = x.astype(jnp.float32)
    y = xf * lax.rsqrt(jnp.mean(xf * xf, axis=-1, keepdims=True) + EPS)
    return (y * gain.astype(jnp.float32)).astype(x.dtype)


def layer_norm(x, gain, bias):
    xf = x.astype(jnp.float32)
    mu = jnp.mean(xf, axis=-1, keepdims=True)
    xc = xf - mu
    y = xc * lax.rsqrt(jnp.mean(xc * xc, axis=-1, keepdims=True) + EPS)
    return (y * gain.astype(jnp.float32) + bias.astype(jnp.float32)).astype(x.dtype)


def conformer_conv(h, w_pw1, b_pw1, w_dw, b_dw, ln_g, ln_b, w_pw2, b_pw2):
    u = h @ w_pw1 + b_pw1
    a, g = jnp.split(u, 2, axis=-1)
    u = a * jax.nn.sigmoid(g)
    u = lax.conv_general_dilated(
        u, w_dw[:, None, :], window_strides=(1,),
        padding=((CONV_WIDTH - 1, 0),),
        dimension_numbers=("NWC", "WIO", "NWC"),
        feature_group_count=D_MODEL) + b_dw
    u = jax.nn.silu(layer_norm(u, ln_g, ln_b))
    return u @ w_pw2 + b_pw2


def gla_mixer(h, w_in, w_gate_up, b_gate, norm_g, w_out):
    bsz, L, _ = h.shape
    proj = h @ w_in
    q, k, v, r, a = jnp.split(
        proj, [GLA_DK, 2 * GLA_DK, 2 * GLA_DK + GLA_DV, 2 * GLA_DK + 2 * GLA_DV], axis=-1)
    g = jax.nn.log_sigmoid((a @ w_gate_up + b_gate).astype(jnp.float32)) / GLA_GATE_TAU
    nc = L // GLA_CHUNK

    def to_chunks(t, dh):
        t = t.astype(jnp.float32).reshape(bsz, nc, GLA_CHUNK, GLA_HEADS, dh)
        return t.transpose(1, 0, 3, 2, 4)

    qc = to_chunks(q * (GLA_HEAD_K ** -0.5), GLA_HEAD_K)
    kc = to_chunks(k, GLA_HEAD_K)
    vc = to_chunks(v, GLA_HEAD_V)
    gc = to_chunks(g, GLA_HEAD_K)
    causal = jnp.tril(jnp.ones((GLA_CHUNK, GLA_CHUNK), dtype=bool))[:, :, None]

    def step(S, inp):
        qb, kb, vb, gb = inp
        b = jnp.cumsum(gb, axis=2)
        o_inter = jnp.einsum("bhcd,bhde->bhce", qb * jnp.exp(b), S)
        diff = jnp.where(causal, b[:, :, :, None, :] - b[:, :, None, :, :], -jnp.inf)
        scores = jnp.einsum("bhid,bhjd,bhijd->bhij", qb, kb, jnp.exp(diff))
        o_intra = jnp.einsum("bhij,bhje->bhie", scores, vb)
        b_last = b[:, :, -1:, :]
        S_new = S * jnp.exp(b_last[:, :, 0, :])[..., None] + jnp.einsum(
            "bhcd,bhce->bhde", kb * jnp.exp(b_last - b), vb)
        return S_new, o_inter + o_intra

    S0 = jnp.zeros((bsz, GLA_HEADS, GLA_HEAD_K, GLA_HEAD_V), jnp.float32)
    _, o = lax.scan(step, S0, (qc, kc, vc, gc))
    o = o.transpose(1, 0, 3, 2, 4).reshape(bsz, L, GLA_HEADS, GLA_HEAD_V)
    o = rms_norm(o, norm_g.reshape(GLA_HEADS, GLA_HEAD_V))
    o = o.reshape(bsz, L, GLA_DV).astype(h.dtype) * jax.nn.silu(r)
    return o @ w_out


def swiglu_ffn(h, w_in, w_out):
    gate, up = jnp.split(h @ w_in, 2, axis=-1)
    return (jax.nn.silu(gate) * up) @ w_out


def setup_inputs(seed: int = 0) -> dict:
    key = jax.random.key(seed)
    ks = iter(jax.random.split(key, 32))
    D = D_MODEL

    def nrm(shape, scale):
        return jax.random.normal(next(ks), shape, jnp.float32) * scale

    def gain(shape):
        return 1.0 + nrm(shape, 0.02)

    return {
        "x": nrm((BATCH, SEQ, D), 1.0),
        "c": nrm((BATCH, D), 1.0),
        "w_ada": nrm((DEPTH, D, N_MOD * D), D ** -0.5),
        "b_ada": nrm((DEPTH, N_MOD * D), 0.02),
        "pre_mix_g": gain((DEPTH, D)),
        "post_mix_g": gain((DEPTH, D)),
        "pre_ffn_g": gain((DEPTH, D)),
        "post_ffn_g": gain((DEPTH, D)),
        "conv_w_pw1": nrm((N_CONV_LAYERS, D, 2 * D), D ** -0.5),
        "conv_b_pw1": nrm((N_CONV_LAYERS, 2 * D), 0.02),
        "conv_w_dw": nrm((N_CONV_LAYERS, CONV_WIDTH, D), CONV_WIDTH ** -0.5),
        "conv_b_dw": nrm((N_CONV_LAYERS, D), 0.02),
        "conv_ln_g": gain((N_CONV_LAYERS, D)),
        "conv_ln_b": nrm((N_CONV_LAYERS, D), 0.02),
        "conv_w_pw2": nrm((N_CONV_LAYERS, D, D), D ** -0.5),
        "conv_b_pw2": nrm((N_CONV_LAYERS, D), 0.02),
        "gla_w_in": nrm((N_GLA_LAYERS, D, GLA_IN_WIDTH), D ** -0.5),
        "gla_w_gate_up": nrm((N_GLA_LAYERS, GLA_GATE_RANK, GLA_DK), GLA_GATE_RANK ** -0.5),
        "gla_b_gate": nrm((N_GLA_LAYERS, GLA_DK), 0.02),
        "gla_norm_g": gain((N_GLA_LAYERS, GLA_DV)),
        "gla_w_out": nrm((N_GLA_LAYERS, GLA_DV, D), GLA_DV ** -0.5),
        "ffn_w_in": nrm((DEPTH, D, 2 * D_FF), D ** -0.5),
        "ffn_w_out": nrm((DEPTH, D_FF, D), D_FF ** -0.5),
    }


def reference(x, c, w_ada, b_ada, pre_mix_g, post_mix_g, pre_ffn_g, post_ffn_g,
              conv_w_pw1, conv_b_pw1, conv_w_dw, conv_b_dw, conv_ln_g, conv_ln_b,
              conv_w_pw2, conv_b_pw2,
              gla_w_in, gla_w_gate_up, gla_b_gate, gla_norm_g, gla_w_out,
              ffn_w_in, ffn_w_out):
    c_act = jax.nn.silu(c)
    for i in range(DEPTH):
        mod = c_act @ w_ada[i] + b_ada[i]
        sh1, sc1, gt1, sh2, sc2, gt2 = jnp.split(mod[:, None, :], N_MOD, axis=-1)

        h = rms_norm(x, pre_mix_g[i]) * (1.0 + sc1) + sh1
        j = i // N_MIXERS
        if i % N_MIXERS == 0:
            y = conformer_conv(h, conv_w_pw1[j], conv_b_pw1[j], conv_w_dw[j], conv_b_dw[j],
                               conv_ln_g[j], conv_ln_b[j], conv_w_pw2[j], conv_b_pw2[j])
        else:
            y = gla_mixer(h, gla_w_in[j], gla_w_gate_up[j], gla_b_gate[j],
                          gla_norm_g[j], gla_w_out[j])
        x = x + gt1 * rms_norm(y, post_mix_g[i])

        h = rms_norm(x, pre_ffn_g[i]) * (1.0 + sc2) + sh2
        y = swiglu_ffn(h, ffn_w_in[i], ffn_w_out[i])
        x = x + gt2 * rms_norm(y, post_ffn_g[i])
    return x
```

```python
import functools

import jax
import jax.numpy as jnp
from jax import lax
from jax.experimental import pallas as pl
from jax.experimental.pallas import tpu as pltpu

F32 = jnp.float32
BF16 = jnp.bfloat16

EPS = 1e-6
CONV_WIDTH = 31
GLA_HEADS = 4
GLA_GATE_RANK = 16
GLA_GATE_TAU = 16.0
N_MOD = 6

LANES = 128
HALO = 32
GLA_CHUNK = 128
GLA_SUB = 16
VMEM_LIMIT = 56 * 1024 * 1024

TN_MOD = 1024
TM_PW1, TN_PW1 = 1024, 512
TM_GLA_IN, TN_GLA_IN = 1024, 896
TM_CONV = 512
TM_GLA_OUT = 512
TM_FFN, TF_FFN = 512, 512


def _params(sem):
    return pltpu.CompilerParams(dimension_semantics=sem, vmem_limit_bytes=VMEM_LIMIT)


def _rms_rows(x):
    return x * lax.rsqrt(jnp.mean(x * x, axis=-1, keepdims=True) + EPS)


def _pre_norm_mod(x, gain, scale, shift):
    return (_rms_rows(x) * gain) * (1.0 + scale) + shift


def _post_norm_residual(x, y, gain, gate):
    return x + gate * (_rms_rows(y) * gain)


def _mod_kernel(c_ref, w_ref, b_ref, o_ref):
    c = c_ref[...]
    c_act = (c * jax.nn.sigmoid(c)).astype(BF16)
    o_ref[0] = jnp.dot(c_act, w_ref[0].astype(BF16), preferred_element_type=F32) + b_ref[0]


def _modulation(c_pad, w_ada, b_ada):
    tn = TN_MOD
    depth, d, n = w_ada.shape
    rows = c_pad.shape[0]
    return pl.pallas_call(
        _mod_kernel,
        grid=(depth, n // tn),
        in_specs=[
            pl.BlockSpec((rows, d), lambda i, j: (0, 0)),
            pl.BlockSpec((1, d, tn), lambda i, j: (i, 0, j)),
            pl.BlockSpec((1, 1, tn), lambda i, j: (i, 0, j)),
        ],
        out_specs=pl.BlockSpec((1, rows, tn), lambda i, j: (i, 0, j)),
        out_shape=jax.ShapeDtypeStruct((depth, rows, n), F32),
        compiler_params=_params(("parallel", "parallel")),
        name="modulation",
    )(c_pad, w_ada, b_ada.reshape(depth, 1, n))


def _prenorm_to_scratch(x_ref, g_ref, sc_ref, sh_ref, h_ref):
    @pl.when(pl.program_id(1) == 0)
    def _():
        h = _pre_norm_mod(x_ref[...], g_ref[...], sc_ref[0], sh_ref[0])
        h_ref[...] = h.astype(BF16)


def _pm_glu_kernel(x_ref, g_ref, sc_ref, sh_ref, wa_ref, wg_ref, ba_ref, bg_ref, o_ref, h_ref):
    _prenorm_to_scratch(x_ref, g_ref, sc_ref, sh_ref, h_ref)
    h = h_ref[...]
    a = jnp.dot(h, wa_ref[...], preferred_element_type=F32) + ba_ref[...]
    g = jnp.dot(h, wg_ref[...], preferred_element_type=F32) + bg_ref[...]
    o_ref[...] = (a * jax.nn.sigmoid(g)).astype(o_ref.dtype)


def _pm_plain_kernel(x_ref, g_ref, sc_ref, sh_ref, w_ref, o_ref, h_ref):
    _prenorm_to_scratch(x_ref, g_ref, sc_ref, sh_ref, h_ref)
    o_ref[...] = jnp.dot(h_ref[...], w_ref[...], preferred_element_type=F32).astype(o_ref.dtype)


def _row_specs(tm, d, tiles_per_batch):
    return [
        pl.BlockSpec((tm, d), lambda i, j: (i, 0)),
        pl.BlockSpec((1, d), lambda i, j: (0, 0)),
        pl.BlockSpec((1, 1, d), lambda i, j: (i // tiles_per_batch, 0, 0)),
        pl.BlockSpec((1, 1, d), lambda i, j: (i // tiles_per_batch, 0, 0)),
    ]


def _prenorm_glu(x, gain, scale, shift, w, bias, seq):
    tm, tn = TM_PW1, TN_PW1
    m, d = x.shape
    n = w.shape[1] // 2
    nb = n // tn
    return pl.pallas_call(
        _pm_glu_kernel,
        grid=(m // tm, nb),
        in_specs=_row_specs(tm, d, seq // tm) + [
            pl.BlockSpec((d, tn), lambda i, j: (0, j)),
            pl.BlockSpec((d, tn), lambda i, j: (0, j + nb)),
            pl.BlockSpec((1, tn), lambda i, j: (0, j)),
            pl.BlockSpec((1, tn), lambda i, j: (0, j + nb)),
        ],
        out_specs=pl.BlockSpec((tm, tn), lambda i, j: (i, j)),
        out_shape=jax.ShapeDtypeStruct((m, n), F32),
        scratch_shapes=[pltpu.VMEM((tm, d), BF16)],
        compiler_params=_params(("parallel", "arbitrary")),
        name="prenorm_pw1_glu",
    )(x, gain, scale, shift, w, w, bias, bias)


def _prenorm_matmul(x, gain, scale, shift, w, seq):
    tm, tn = TM_GLA_IN, TN_GLA_IN
    m, d = x.shape
    n = w.shape[1]
    return pl.pallas_call(
        _pm_plain_kernel,
        grid=(m // tm, n // tn),
        in_specs=_row_specs(tm, d, seq // tm) + [pl.BlockSpec((d, tn), lambda i, j: (0, j))],
        out_specs=pl.BlockSpec((tm, tn), lambda i, j: (i, j)),
        out_shape=jax.ShapeDtypeStruct((m, n), BF16),
        scratch_shapes=[pltpu.VMEM((tm, d), BF16)],
        compiler_params=_params(("parallel", "arbitrary")),
        name="prenorm_gla_in",
    )(x, gain, scale, shift, w)


CONV_ROWS = 64
CONV_COLS = 256


def _conv_tail_kernel(u_ref, halo_ref, x_ref, wdw_ref, bdw_ref, lng_ref, lnb_ref, w2_ref, b2_ref,
                      pg_ref, gt_ref, o_ref, ext_ref, cv_ref, *, tiles_per_batch):
    tm, d = u_ref.shape
    first = (pl.program_id(0) % tiles_per_batch) == 0
    ext_ref[0:HALO, :] = jnp.where(first, 0.0, halo_ref[...])
    ext_ref[HALO:, :] = u_ref[...]

    base = HALO - (CONV_WIDTH - 1)
    for c in range(d // CONV_COLS):
        cols = slice(c * CONV_COLS, (c + 1) * CONV_COLS)

        def row_body(r, carry, cols=cols):
            r0 = pl.multiple_of(r * CONV_ROWS, CONV_ROWS)
            acc = jnp.broadcast_to(bdw_ref[:, cols], (CONV_ROWS, CONV_COLS))
            win = ext_ref[pl.ds(r0, CONV_ROWS + HALO), cols]
            for res in range(8):
                shifted = win if res == 0 else pltpu.roll(win, CONV_ROWS + HALO - res, 0)
                for off in range(res, HALO + 1, 8):
                    k = off - base
                    if 0 <= k < CONV_WIDTH:
                        acc = acc + wdw_ref[k:k + 1, cols] * shifted[off - res:off - res + CONV_ROWS]
            cv_ref[pl.ds(r0, CONV_ROWS), cols] = acc
            return carry

        lax.fori_loop(0, tm // CONV_ROWS, row_body, 0)

    cv = cv_ref[...]
    mu = jnp.mean(cv, axis=-1, keepdims=True)
    xc = cv - mu
    ln = xc * lax.rsqrt(jnp.mean(xc * xc, axis=-1, keepdims=True) + EPS) * lng_ref[...] + lnb_ref[...]
    v = (ln * jax.nn.sigmoid(ln)).astype(BF16)
    y = jnp.dot(v, w2_ref[...], preferred_element_type=F32) + b2_ref[...]
    o_ref[...] = _post_norm_residual(x_ref[...], y, pg_ref[...], gt_ref[0])


def _conv_tail(u, x, w_dw, b_dw, ln_g, ln_b, w2, b2, post_g, gate, seq):
    tm = TM_CONV
    m, d = u.shape
    tiles_per_batch = seq // tm
    halo_blocks = tm // HALO
    vec = pl.BlockSpec((1, d), lambda i: (0, 0))
    return pl.pallas_call(
        functools.partial(_conv_tail_kernel, tiles_per_batch=tiles_per_batch),
        grid=(m // tm,),
        in_specs=[
            pl.BlockSpec((tm, d), lambda i: (i, 0)),
            pl.BlockSpec((HALO, d), lambda i: (jnp.maximum(i * halo_blocks - 1, 0), 0)),
            pl.BlockSpec((tm, d), lambda i: (i, 0)),
            pl.BlockSpec(w_dw.shape, lambda i: (0, 0)),
            vec, vec, vec,
            pl.BlockSpec(w2.shape, lambda i: (0, 0)),
            vec, vec,
            pl.BlockSpec((1, 1, d), lambda i: (i // tiles_per_batch, 0, 0)),
        ],
        out_specs=pl.BlockSpec((tm, d), lambda i: (i, 0)),
        out_shape=jax.ShapeDtypeStruct((m, d), F32),
        scratch_shapes=[pltpu.VMEM((tm + HALO, d), F32), pltpu.VMEM((tm, d), F32)],
        compiler_params=_params(("arbitrary",)),
        name="conv_tail",
    )(u, u, x, w_dw, b_dw, ln_g, ln_b, w2, b2, post_g, gate)


def _cumsum_rows(g):
    n = g.shape[0]
    row = lax.broadcasted_iota(jnp.int32, g.shape, 0)
    shift = 1
    while shift < n:
        g = g + jnp.where(row >= shift, pltpu.roll(g, shift, 0), 0.0)
        shift *= 2
    return g


def _gla_kernel(q_ref, k_ref, v_ref, r_ref, a_ref, wg_ref, bg_ref, ng_ref, o_ref, s_ref, *, scale):
    c = q_ref.shape[1]
    dk = q_ref.shape[2]

    @pl.when(pl.program_id(2) == 0)
    def _():
        s_ref[...] = jnp.zeros_like(s_ref)

    q = q_ref[0].astype(F32)
    k = k_ref[0].astype(F32)
    v = v_ref[0]

    x = jnp.dot(a_ref[0], wg_ref[...], preferred_element_type=F32) + bg_ref[...]
    g = (jnp.minimum(x, 0.0) - jnp.log1p(jnp.exp(-jnp.abs(x)))) * (1.0 / GLA_GATE_TAU)
    b = _cumsum_rows(g)
    b_last = b[c - 1:c, :]

    s = s_ref[...]
    o = jnp.dot((q * jnp.exp(b)).astype(BF16), s.astype(BF16), preferred_element_type=F32)

    row = lax.broadcasted_iota(jnp.int32, (c, c), 0)
    col = lax.broadcasted_iota(jnp.int32, (c, c), 1)
    blocks = [jnp.zeros((GLA_SUB, c), F32)]
    for i in range(1, c // GLA_SUB):
        rows = slice(i * GLA_SUB, (i + 1) * GLA_SUB)
        ref = b[i * GLA_SUB:i * GLA_SUB + 1, :]
        qt = (q[rows] * jnp.exp(b[rows] - ref)).astype(BF16)
        kt = (k * jnp.exp(jnp.minimum(ref - b, 0.0))).astype(BF16)
        blocks.append(lax.dot_general(qt, kt, (((1,), (1,)), ((), ())), preferred_element_type=F32))
    scores = jnp.where(col < (row // GLA_SUB) * GLA_SUB, jnp.concatenate(blocks, axis=0), 0.0)

    ones = jnp.ones((dk, LANES), BF16)
    diag = row - col
    sub_row = row % GLA_SUB
    for delta in range(GLA_SUB):
        if delta == 0:
            t = q * k
        else:
            t = q * pltpu.roll(k, delta, 0) * jnp.exp(b - pltpu.roll(b, delta, 0))
        rowsum = jnp.dot(t.astype(BF16), ones, preferred_element_type=F32)
        scores = jnp.where((diag == delta) & (sub_row >= delta), rowsum, scores)

    o = (o + jnp.dot(scores.astype(BF16), v, preferred_element_type=F32)) * scale

    ke_t = jnp.transpose(k * jnp.exp(b_last - b)).astype(BF16)
    decay = jnp.transpose(jnp.broadcast_to(jnp.exp(b_last), (LANES, dk)))[:, 0:1]
    s_ref[...] = decay * s + jnp.dot(ke_t, v, preferred_element_type=F32)

    r = r_ref[0].astype(F32)
    o_ref[0] = (_rms_rows(o) * ng_ref[...] * (r * jax.nn.sigmoid(r))).astype(o_ref.dtype)


def _gla_attention(proj, w_gate, b_gate, norm_g, dk, dv):
    bsz, seq, _ = proj.shape
    hk, hv = dk // GLA_HEADS, dv // GLA_HEADS
    c = GLA_CHUNK
    k_off, v_off, r_off, a_off = dk // hk, 2 * dk // hv, (2 * dk + dv) // hv, (2 * dk + 2 * dv) // LANES
    return pl.pallas_call(
        functools.partial(_gla_kernel, scale=hk ** -0.5),
        grid=(bsz, GLA_HEADS, seq // c),
        in_specs=[
            pl.BlockSpec((1, c, hk), lambda b, h, t: (b, t, h)),
            pl.BlockSpec((1, c, hk), lambda b, h, t: (b, t, k_off + h)),
            pl.BlockSpec((1, c, hv), lambda b, h, t: (b, t, v_off + h)),
            pl.BlockSpec((1, c, hv), lambda b, h, t: (b, t, r_off + h)),
            pl.BlockSpec((1, c, LANES), lambda b, h, t: (b, t, a_off)),
            pl.BlockSpec((LANES, hk), lambda b, h, t: (0, h)),
            pl.BlockSpec((1, hk), lambda b, h, t: (0, h)),
            pl.BlockSpec((1, hv), lambda b, h, t: (0, h)),
        ],
        out_specs=pl.BlockSpec((1, c, hv), lambda b, h, t: (b, t, h)),
        out_shape=jax.ShapeDtypeStruct((bsz, seq, dv), BF16),
        scratch_shapes=[pltpu.VMEM((hk, hv), F32)],
        compiler_params=_params(("parallel", "parallel", "arbitrary")),
        name="gla_attention",
    )(proj, proj, proj, proj, proj, w_gate, b_gate, norm_g)


def _mm_post_kernel(a_ref, w_ref, x_ref, pg_ref, gt_ref, o_ref):
    y = jnp.dot(a_ref[...], w_ref[...], preferred_element_type=F32)
    o_ref[...] = _post_norm_residual(x_ref[...], y, pg_ref[...], gt_ref[0])


def _matmul_post(a, w, x, post_g, gate, seq):
    tm = TM_GLA_OUT
    m, kdim = a.shape
    d = w.shape[1]
    tiles_per_batch = seq // tm
    return pl.pallas_call(
        _mm_post_kernel,
        grid=(m // tm,),
        in_specs=[
            pl.BlockSpec((tm, kdim), lambda i: (i, 0)),
            pl.BlockSpec((kdim, d), lambda i: (0, 0)),
            pl.BlockSpec((tm, d), lambda i: (i, 0)),
            pl.BlockSpec((1, d), lambda i: (0, 0)),
            pl.BlockSpec((1, 1, d), lambda i: (i // tiles_per_batch, 0, 0)),
        ],
        out_specs=pl.BlockSpec((tm, d), lambda i: (i, 0)),
        out_shape=jax.ShapeDtypeStruct((m, d), F32),
        compiler_params=_params(("parallel",)),
        name="gla_out_post",
    )(a, w, x, post_g, gate)


def _ffn_kernel(x_ref, g_ref, sc_ref, sh_ref, wg_ref, wu_ref, wo_ref, pg_ref, gt_ref, o_ref, h_ref, acc_ref):
    f = pl.program_id(1)
    _prenorm_to_scratch(x_ref, g_ref, sc_ref, sh_ref, h_ref)
    h = h_ref[...]
    gate = jnp.dot(h, wg_ref[...], preferred_element_type=F32)
    up = jnp.dot(h, wu_ref[...], preferred_element_type=F32)
    act = (gate * jax.nn.sigmoid(gate) * up).astype(BF16)
    part = jnp.dot(act, wo_ref[...], preferred_element_type=F32)

    @pl.when(f == 0)
    def _():
        acc_ref[...] = part

    @pl.when(f > 0)
    def _():
        acc_ref[...] += part

    @pl.when(f == pl.num_programs(1) - 1)
    def _():
        o_ref[...] = _post_norm_residual(x_ref[...], acc_ref[...], pg_ref[...], gt_ref[0])


def _ffn(x, gain, scale, shift, w_in, w_out, post_g, gate, seq):
    tm, tf = TM_FFN, TF_FFN
    m, d = x.shape
    dff = w_out.shape[0]
    nf = dff // tf
    tiles_per_batch = seq // tm
    return pl.pallas_call(
        _ffn_kernel,
        grid=(m // tm, nf),
        in_specs=_row_specs(tm, d, tiles_per_batch) + [
            pl.BlockSpec((d, tf), lambda i, f: (0, f)),
            pl.BlockSpec((d, tf), lambda i, f: (0, f + nf)),
            pl.BlockSpec((tf, d), lambda i, f: (f, 0)),
            pl.BlockSpec((1, d), lambda i, f: (0, 0)),
            pl.BlockSpec((1, 1, d), lambda i, f: (i // tiles_per_batch, 0, 0)),
        ],
        out_specs=pl.BlockSpec((tm, d), lambda i, f: (i, 0)),
        out_shape=jax.ShapeDtypeStruct((m, d), F32),
        scratch_shapes=[pltpu.VMEM((tm, d), BF16), pltpu.VMEM((tm, d), F32)],
        compiler_params=_params(("parallel", "arbitrary")),
        name="ffn",
    )(x, gain, scale, shift, w_in, w_in, w_out, post_g, gate)


def kernel(x, c, w_ada, b_ada, pre_mix_g, post_mix_g, pre_ffn_g, post_ffn_g, conv_w_pw1, conv_b_pw1, conv_w_dw, conv_b_dw, conv_ln_g, conv_ln_b, conv_w_pw2, conv_b_pw2, gla_w_in, gla_w_gate_up, gla_b_gate, gla_norm_g, gla_w_out, ffn_w_in, ffn_w_out):
    bsz, seq, d = x.shape
    depth = w_ada.shape[0]
    dk = gla_w_gate_up.shape[2]
    dv = gla_norm_g.shape[1]
    rank = gla_w_gate_up.shape[1]

    c_pad = jnp.zeros((8, d), F32).at[:bsz].set(c)
    mod = _modulation(c_pad, w_ada, b_ada)[:, :bsz]
    mod = mod.reshape(depth, bsz, N_MOD, 1, d)

    def row(vec):
        return vec.reshape(1, -1)

    xf = x.reshape(bsz * seq, d)
    for i in range(depth):
        sh1, sc1, gt1, sh2, sc2, gt2 = (mod[i, :, n] for n in range(N_MOD))
        j = i // 2
        if i % 2 == 0:
            u = _prenorm_glu(xf, row(pre_mix_g[i]), sc1, sh1, conv_w_pw1[j].astype(BF16),
                             row(conv_b_pw1[j]), seq)
            w_dw = jnp.zeros((HALO, d), F32).at[:CONV_WIDTH].set(conv_w_dw[j])
            xf = _conv_tail(u, xf, w_dw, row(conv_b_dw[j]), row(conv_ln_g[j]), row(conv_ln_b[j]),
                            conv_w_pw2[j].astype(BF16), row(conv_b_pw2[j]), row(post_mix_g[i]), gt1, seq)
        else:
            w_in = jnp.pad(gla_w_in[j], ((0, 0), (0, LANES - rank))).astype(BF16)
            w_gate = jnp.pad(gla_w_gate_up[j], ((0, LANES - rank), (0, 0))).astype(BF16)
            proj = _prenorm_matmul(xf, row(pre_mix_g[i]), sc1, sh1, w_in, seq)
            o = _gla_attention(proj.reshape(bsz, seq, -1), w_gate, row(gla_b_gate[j]), row(gla_norm_g[j]), dk, dv)
            xf = _matmul_post(o.reshape(bsz * seq, dv), gla_w_out[j].astype(BF16), xf, row(post_mix_g[i]), gt1, seq)
        xf = _ffn(xf, row(pre_ffn_g[i]), sc2, sh2, ffn_w_in[i].astype(BF16), ffn_w_out[i].astype(BF16),
                  row(post_ffn_g[i]), gt2, seq)
    return xf.reshape(bsz, seq, d)
```

```python
import functools

import jax
import jax.numpy as jnp
from jax import lax
from jax.experimental import pallas as pl
from jax.experimental.pallas import tpu as pltpu

F32 = jnp.float32
BF16 = jnp.bfloat16

EPS = 1e-6
CONV_WIDTH = 31
GLA_HEADS = 4
GLA_GATE_RANK = 16
GLA_GATE_TAU = 16.0
N_MOD = 6

LANES = 128
HALO = 32
GLA_CHUNK = 128
GLA_SUB = 16
VMEM_LIMIT = 56 * 1024 * 1024

TN_MOD = 1024
TM_PW1, TN_PW1 = 1024, 512
TM_GLA_IN, TN_GLA_IN = 1024, 896
TM_CONV = 512
TM_GLA_OUT = 512
TM_FFN, TF_FFN = 512, 512


def _params(sem):
    return pltpu.CompilerParams(dimension_semantics=sem, vmem_limit_bytes=VMEM_LIMIT)


def _rms_rows(x):
    return x * lax.rsqrt(jnp.mean(x * x, axis=-1, keepdims=True) + EPS)


def _pre_norm_mod(x, gain, scale, shift):
    return (_rms_rows(x) * gain) * (1.0 + scale) + shift


def _post_norm_residual(x, y, gain, gate):
    return x + gate * (_rms_rows(y) * gain)


def _mod_kernel(c_ref, w_ref, b_ref, o_ref):
    c = c_ref[...]
    c_act = (c * jax.nn.sigmoid(c)).astype(BF16)
    o_ref[0] = jnp.dot(c_act, w_ref[0].astype(BF16), preferred_element_type=F32) + b_ref[0]


def _modulation(c_pad, w_ada, b_ada):
    tn = TN_MOD
    depth, d, n = w_ada.shape
    rows = c_pad.shape[0]
    return pl.pallas_call(
        _mod_kernel,
        grid=(depth, n // tn),
        in_specs=[
            pl.BlockSpec((rows, d), lambda i, j: (0, 0)),
            pl.BlockSpec((1, d, tn), lambda i, j: (i, 0, j)),
            pl.BlockSpec((1, 1, tn), lambda i, j: (i, 0, j)),
        ],
        out_specs=pl.BlockSpec((1, rows, tn), lambda i, j: (i, 0, j)),
        out_shape=jax.ShapeDtypeStruct((depth, rows, n), F32),
        compiler_params=_params(("parallel", "parallel")),
        name="modulation",
    )(c_pad, w_ada, b_ada.reshape(depth, 1, n))


def _prenorm_to_scratch(x_ref, g_ref, sc_ref, sh_ref, h_ref):
    @pl.when(pl.program_id(1) == 0)
    def _():
        h = _pre_norm_mod(x_ref[...], g_ref[...], sc_ref[0], sh_ref[0])
        h_ref[...] = h.astype(BF16)


def _pm_glu_kernel(x_ref, g_ref, sc_ref, sh_ref, wa_ref, wg_ref, ba_ref, bg_ref, o_ref, h_ref):
    _prenorm_to_scratch(x_ref, g_ref, sc_ref, sh_ref, h_ref)
    h = h_ref[...]
    a = jnp.dot(h, wa_ref[...], preferred_element_type=F32) + ba_ref[...]
    g = jnp.dot(h, wg_ref[...], preferred_element_type=F32) + bg_ref[...]
    o_ref[...] = (a * jax.nn.sigmoid(g)).astype(o_ref.dtype)


def _pm_plain_kernel(x_ref, g_ref, sc_ref, sh_ref, w_ref, o_ref, h_ref):
    _prenorm_to_scratch(x_ref, g_ref, sc_ref, sh_ref, h_ref)
    o_ref[...] = jnp.dot(h_ref[...], w_ref[...], preferred_element_type=F32).astype(o_ref.dtype)


def _row_specs(tm, d, tiles_per_batch):
    return [
        pl.BlockSpec((tm, d), lambda i, j: (i, 0)),
        pl.BlockSpec((1, d), lambda i, j: (0, 0)),
        pl.BlockSpec((1, 1, d), lambda i, j: (i // tiles_per_batch, 0, 0)),
        pl.BlockSpec((1, 1, d), lambda i, j: (i // tiles_per_batch, 0, 0)),
    ]


def _prenorm_glu(x, gain, scale, shift, w, layer, bias, seq):
    tm, tn = TM_PW1, TN_PW1
    m, d = x.shape
    n = w.shape[2] // 2
    nb = n // tn
    return pl.pallas_call(
        _pm_glu_kernel,
        grid=(m // tm, nb),
        in_specs=_row_specs(tm, d, seq // tm) + [
            pl.BlockSpec((None, d, tn), lambda i, j: (layer, 0, j)),
            pl.BlockSpec((None, d, tn), lambda i, j: (layer, 0, j + nb)),
            pl.BlockSpec((1, tn), lambda i, j: (0, j)),
            pl.BlockSpec((1, tn), lambda i, j: (0, j + nb)),
        ],
        out_specs=pl.BlockSpec((tm, tn), lambda i, j: (i, j)),
        out_shape=jax.ShapeDtypeStruct((m, n), F32),
        scratch_shapes=[pltpu.VMEM((tm, d), BF16)],
        compiler_params=_params(("parallel", "arbitrary")),
        name="prenorm_pw1_glu",
    )(x, gain, scale, shift, w, w, bias, bias)


def _prenorm_matmul(x, gain, scale, shift, w, layer, seq):
    tm, tn = TM_GLA_IN, TN_GLA_IN
    m, d = x.shape
    n = w.shape[2]
    return pl.pallas_call(
        _pm_plain_kernel,
        grid=(m // tm, n // tn),
        in_specs=_row_specs(tm, d, seq // tm) + [pl.BlockSpec((None, d, tn), lambda i, j: (layer, 0, j))],
        out_specs=pl.BlockSpec((tm, tn), lambda i, j: (i, j)),
        out_shape=jax.ShapeDtypeStruct((m, n), BF16),
        scratch_shapes=[pltpu.VMEM((tm, d), BF16)],
        compiler_params=_params(("parallel", "arbitrary")),
        name="prenorm_gla_in",
    )(x, gain, scale, shift, w)


CONV_ROWS = 64
CONV_COLS = 256


def _conv_tail_kernel(u_ref, halo_ref, x_ref, wdw_ref, bdw_ref, lng_ref, lnb_ref, w2_ref, b2_ref,
                      pg_ref, gt_ref, o_ref, ext_ref, cv_ref, *, tiles_per_batch):
    tm, d = u_ref.shape
    first = (pl.program_id(0) % tiles_per_batch) == 0
    ext_ref[0:HALO, :] = jnp.where(first, 0.0, halo_ref[...])
    ext_ref[HALO:, :] = u_ref[...]

    base = HALO - (CONV_WIDTH - 1)
    for c in range(d // CONV_COLS):
        cols = slice(c * CONV_COLS, (c + 1) * CONV_COLS)

        def row_body(r, carry, cols=cols):
            r0 = pl.multiple_of(r * CONV_ROWS, CONV_ROWS)
            acc = jnp.broadcast_to(bdw_ref[:, cols], (CONV_ROWS, CONV_COLS))
            win = ext_ref[pl.ds(r0, CONV_ROWS + HALO), cols]
            for res in range(8):
                shifted = win if res == 0 else pltpu.roll(win, CONV_ROWS + HALO - res, 0)
                for off in range(res, HALO + 1, 8):
                    k = off - base
                    if 0 <= k < CONV_WIDTH:
                        acc = acc + wdw_ref[k:k + 1, cols] * shifted[off - res:off - res + CONV_ROWS]
            cv_ref[pl.ds(r0, CONV_ROWS), cols] = acc
            return carry

        lax.fori_loop(0, tm // CONV_ROWS, row_body, 0)

    cv = cv_ref[...]
    mu = jnp.mean(cv, axis=-1, keepdims=True)
    xc = cv - mu
    ln = xc * lax.rsqrt(jnp.mean(xc * xc, axis=-1, keepdims=True) + EPS) * lng_ref[...] + lnb_ref[...]
    v = (ln * jax.nn.sigmoid(ln)).astype(BF16)
    y = jnp.dot(v, w2_ref[...], preferred_element_type=F32) + b2_ref[...]
    o_ref[...] = _post_norm_residual(x_ref[...], y, pg_ref[...], gt_ref[0])


def _conv_tail(u, x, w_dw, b_dw, ln_g, ln_b, w2, layer, b2, post_g, gate, seq):
    tm = TM_CONV
    m, d = u.shape
    tiles_per_batch = seq // tm
    halo_blocks = tm // HALO
    vec = pl.BlockSpec((1, d), lambda i: (0, 0))
    return pl.pallas_call(
        functools.partial(_conv_tail_kernel, tiles_per_batch=tiles_per_batch),
        grid=(m // tm,),
        in_specs=[
            pl.BlockSpec((tm, d), lambda i: (i, 0)),
            pl.BlockSpec((HALO, d), lambda i: (jnp.maximum(i * halo_blocks - 1, 0), 0)),
            pl.BlockSpec((tm, d), lambda i: (i, 0)),
            pl.BlockSpec(w_dw.shape, lambda i: (0, 0)),
            vec, vec, vec,
            pl.BlockSpec((None,) + w2.shape[1:], lambda i: (layer, 0, 0)),
            vec, vec,
            pl.BlockSpec((1, 1, d), lambda i: (i // tiles_per_batch, 0, 0)),
        ],
        out_specs=pl.BlockSpec((tm, d), lambda i: (i, 0)),
        out_shape=jax.ShapeDtypeStruct((m, d), F32),
        scratch_shapes=[pltpu.VMEM((tm + HALO, d), F32), pltpu.VMEM((tm, d), F32)],
        compiler_params=_params(("arbitrary",)),
        name="conv_tail",
    )(u, u, x, w_dw, b_dw, ln_g, ln_b, w2, b2, post_g, gate)


def _cumsum_rows(g):
    n = g.shape[0]
    row = lax.broadcasted_iota(jnp.int32, g.shape, 0)
    shift = 1
    while shift < n:
        g = g + jnp.where(row >= shift, pltpu.roll(g, shift, 0), 0.0)
        shift *= 2
    return g


def _gla_kernel(q_ref, k_ref, v_ref, r_ref, a_ref, wg_ref, bg_ref, ng_ref, o_ref, s_ref, *, scale):
    c, dk = q_ref.shape

    @pl.when(pl.program_id(2) == 0)
    def _():
        s_ref[...] = jnp.zeros_like(s_ref)

    q = q_ref[...].astype(F32)
    k = k_ref[...].astype(F32)
    v = v_ref[...]

    x = jnp.dot(a_ref[...], wg_ref[...], preferred_element_type=F32) + bg_ref[...]
    g = (jnp.minimum(x, 0.0) - jnp.log1p(jnp.exp(-jnp.abs(x)))) * (1.0 / GLA_GATE_TAU)
    b = _cumsum_rows(g)
    b_last = b[c - 1:c, :]

    s = s_ref[...]
    o = jnp.dot((q * jnp.exp(b)).astype(BF16), s.astype(BF16), preferred_element_type=F32)

    row = lax.broadcasted_iota(jnp.int32, (c, c), 0)
    col = lax.broadcasted_iota(jnp.int32, (c, c), 1)
    blocks = [jnp.zeros((GLA_SUB, c), F32)]
    for i in range(1, c // GLA_SUB):
        rows = slice(i * GLA_SUB, (i + 1) * GLA_SUB)
        ref = b[i * GLA_SUB:i * GLA_SUB + 1, :]
        qt = (q[rows] * jnp.exp(b[rows] - ref)).astype(BF16)
        kt = (k * jnp.exp(jnp.minimum(ref - b, 0.0))).astype(BF16)
        blocks.append(lax.dot_general(qt, kt, (((1,), (1,)), ((), ())), preferred_element_type=F32))
    scores = jnp.where(col < (row // GLA_SUB) * GLA_SUB, jnp.concatenate(blocks, axis=0), 0.0)

    ones = jnp.ones((dk, LANES), BF16)
    diag = row - col
    sub_row = row % GLA_SUB
    for delta in range(GLA_SUB):
        if delta == 0:
            t = q * k
        else:
            t = q * pltpu.roll(k, delta, 0) * jnp.exp(b - pltpu.roll(b, delta, 0))
        rowsum = jnp.dot(t.astype(BF16), ones, preferred_element_type=F32)
        scores = jnp.where((diag == delta) & (sub_row >= delta), rowsum, scores)

    o = (o + jnp.dot(scores.astype(BF16), v, preferred_element_type=F32)) * scale

    ke_t = jnp.transpose(k * jnp.exp(b_last - b)).astype(BF16)
    decay = jnp.transpose(jnp.broadcast_to(jnp.exp(b_last), (LANES, dk)))[:, 0:1]
    s_ref[...] = decay * s + jnp.dot(ke_t, v, preferred_element_type=F32)

    r = r_ref[...].astype(F32)
    o_ref[...] = (_rms_rows(o) * ng_ref[...] * (r * jax.nn.sigmoid(r))).astype(o_ref.dtype)


def _gla_attention(proj, w_gate, layer, b_gate, norm_g, bsz, dk, dv):
    m = proj.shape[0]
    hk, hv = dk // GLA_HEADS, dv // GLA_HEADS
    c = GLA_CHUNK
    nt = m // bsz // c
    k_off, v_off, r_off, a_off = dk // hk, 2 * dk // hv, (2 * dk + dv) // hv, (2 * dk + 2 * dv) // LANES
    return pl.pallas_call(
        functools.partial(_gla_kernel, scale=hk ** -0.5),
        grid=(bsz, GLA_HEADS, nt),
        in_specs=[
            pl.BlockSpec((c, hk), lambda b, h, t: (b * nt + t, h)),
            pl.BlockSpec((c, hk), lambda b, h, t: (b * nt + t, k_off + h)),
            pl.BlockSpec((c, hv), lambda b, h, t: (b * nt + t, v_off + h)),
            pl.BlockSpec((c, hv), lambda b, h, t: (b * nt + t, r_off + h)),
            pl.BlockSpec((c, LANES), lambda b, h, t: (b * nt + t, a_off)),
            pl.BlockSpec((None, LANES, hk), lambda b, h, t: (layer, 0, h)),
            pl.BlockSpec((1, hk), lambda b, h, t: (0, h)),
            pl.BlockSpec((1, hv), lambda b, h, t: (0, h)),
        ],
        out_specs=pl.BlockSpec((c, hv), lambda b, h, t: (b * nt + t, h)),
        out_shape=jax.ShapeDtypeStruct((m, dv), BF16),
        scratch_shapes=[pltpu.VMEM((hk, hv), F32)],
        compiler_params=_params(("parallel", "parallel", "arbitrary")),
        name="gla_attention",
    )(proj, proj, proj, proj, proj, w_gate, b_gate, norm_g)


def _mm_post_kernel(a_ref, w_ref, x_ref, pg_ref, gt_ref, o_ref):
    y = jnp.dot(a_ref[...], w_ref[...], preferred_element_type=F32)
    o_ref[...] = _post_norm_residual(x_ref[...], y, pg_ref[...], gt_ref[0])


def _matmul_post(a, w, layer, x, post_g, gate, seq):
    tm = TM_GLA_OUT
    m, kdim = a.shape
    d = w.shape[2]
    tiles_per_batch = seq // tm
    return pl.pallas_call(
        _mm_post_kernel,
        grid=(m // tm,),
        in_specs=[
            pl.BlockSpec((tm, kdim), lambda i: (i, 0)),
            pl.BlockSpec((None, kdim, d), lambda i: (layer, 0, 0)),
            pl.BlockSpec((tm, d), lambda i: (i, 0)),
            pl.BlockSpec((1, d), lambda i: (0, 0)),
            pl.BlockSpec((1, 1, d), lambda i: (i // tiles_per_batch, 0, 0)),
        ],
        out_specs=pl.BlockSpec((tm, d), lambda i: (i, 0)),
        out_shape=jax.ShapeDtypeStruct((m, d), F32),
        compiler_params=_params(("parallel",)),
        name="gla_out_post",
    )(a, w, x, post_g, gate)


def _ffn_kernel(x_ref, g_ref, sc_ref, sh_ref, wg_ref, wu_ref, wo_ref, pg_ref, gt_ref, o_ref, h_ref, acc_ref):
    f = pl.program_id(1)
    _prenorm_to_scratch(x_ref, g_ref, sc_ref, sh_ref, h_ref)

    @pl.when(f == 0)
    def _():
        acc_ref[...] = jnp.zeros_like(acc_ref)

    h = h_ref[...]
    gate = jnp.dot(h, wg_ref[...], preferred_element_type=F32)
    up = jnp.dot(h, wu_ref[...], preferred_element_type=F32)
    act = (gate * jax.nn.sigmoid(gate) * up).astype(BF16)
    acc_ref[...] += jnp.dot(act, wo_ref[...], preferred_element_type=F32)

    @pl.when(f == pl.num_programs(1) - 1)
    def _():
        o_ref[...] = _post_norm_residual(x_ref[...], acc_ref[...], pg_ref[...], gt_ref[0])


def _ffn(x, gain, scale, shift, w_in, w_out, layer, post_g, gate, seq):
    tm, tf = TM_FFN, TF_FFN
    m, d = x.shape
    dff = w_out.shape[1]
    nf = dff // tf
    tiles_per_batch = seq // tm
    return pl.pallas_call(
        _ffn_kernel,
        grid=(m // tm, nf),
        in_specs=_row_specs(tm, d, tiles_per_batch) + [
            pl.BlockSpec((None, d, tf), lambda i, f: (layer, 0, f)),
            pl.BlockSpec((None, d, tf), lambda i, f: (layer, 0, f + nf)),
            pl.BlockSpec((None, tf, d), lambda i, f: (layer, f, 0)),
            pl.BlockSpec((1, d), lambda i, f: (0, 0)),
            pl.BlockSpec((1, 1, d), lambda i, f: (i // tiles_per_batch, 0, 0)),
        ],
        out_specs=pl.BlockSpec((tm, d), lambda i, f: (i, 0)),
        out_shape=jax.ShapeDtypeStruct((m, d), F32),
        scratch_shapes=[pltpu.VMEM((tm, d), BF16), pltpu.VMEM((tm, d), F32)],
        compiler_params=_params(("parallel", "arbitrary")),
        name="ffn",
    )(x, gain, scale, shift, w_in, w_in, w_out, post_g, gate)


def kernel(x, c, w_ada, b_ada, pre_mix_g, post_mix_g, pre_ffn_g, post_ffn_g, conv_w_pw1, conv_b_pw1, conv_w_dw, conv_b_dw, conv_ln_g, conv_ln_b, conv_w_pw2, conv_b_pw2, gla_w_in, gla_w_gate_up, gla_b_gate, gla_norm_g, gla_w_out, ffn_w_in, ffn_w_out):
    bsz, seq, d = x.shape
    depth = w_ada.shape[0]
    dk = gla_w_gate_up.shape[2]
    dv = gla_norm_g.shape[1]
    rank = gla_w_gate_up.shape[1]

    c_pad = jnp.zeros((8, d), F32).at[:bsz].set(c)
    mod = _modulation(c_pad, w_ada, b_ada)[:, :bsz]
    mod = mod.reshape(depth, bsz, N_MOD, 1, d)

    def row(vec):
        return vec.reshape(1, -1)

    pw1_w, pw2_w = conv_w_pw1.astype(BF16), conv_w_pw2.astype(BF16)
    gla_in_w = jnp.pad(gla_w_in, ((0, 0), (0, 0), (0, LANES - rank))).astype(BF16)
    gla_gate_w = jnp.pad(gla_w_gate_up, ((0, 0), (0, LANES - rank), (0, 0))).astype(BF16)
    gla_out_w = gla_w_out.astype(BF16)
    ffn_in_w, ffn_out_w = ffn_w_in.astype(BF16), ffn_w_out.astype(BF16)

    xf = x.reshape(bsz * seq, d)
    for i in range(depth):
        sh1, sc1, gt1, sh2, sc2, gt2 = (mod[i, :, n] for n in range(N_MOD))
        j = i // 2
        if i % 2 == 0:
            u = _prenorm_glu(xf, row(pre_mix_g[i]), sc1, sh1, pw1_w, j, row(conv_b_pw1[j]), seq)
            w_dw = jnp.zeros((HALO, d), F32).at[:CONV_WIDTH].set(conv_w_dw[j])
            xf = _conv_tail(u, xf, w_dw, row(conv_b_dw[j]), row(conv_ln_g[j]), row(conv_ln_b[j]),
                            pw2_w, j, row(conv_b_pw2[j]), row(post_mix_g[i]), gt1, seq)
        else:
            proj = _prenorm_matmul(xf, row(pre_mix_g[i]), sc1, sh1, gla_in_w, j, seq)
            o = _gla_attention(proj, gla_gate_w, j, row(gla_b_gate[j]), row(gla_norm_g[j]), bsz, dk, dv)
            xf = _matmul_post(o, gla_out_w, j, xf, row(post_mix_g[i]), gt1, seq)
        xf = _ffn(xf, row(pre_ffn_g[i]), sc2, sh2, ffn_in_w, ffn_out_w, i, row(post_ffn_g[i]), gt2, seq)
    return xf.reshape(bsz, seq, d)
```

```python
import functools

import numpy as np
import jax
import jax.numpy as jnp
from jax import lax
from jax.experimental import pallas as pl
from jax.experimental.pallas import tpu as pltpu

F32 = jnp.float32
BF16 = jnp.bfloat16

EPS = 1e-6
CONV_WIDTH = 31
GLA_HEADS = 4
GLA_GATE_RANK = 16
GLA_GATE_TAU = 16.0
N_MOD = 6

LANES = 128
HALO = 32
GLA_CHUNK = 128
GLA_STEP = 512
SUBLANES = 8
GLA_SMALL = (4, 2, 1)
VMEM_LIMIT = 56 * 1024 * 1024

TN_MOD = 1024
TM_PW1, TN_PW1 = 1024, 512
TM_GLA_IN, TN_GLA_IN = 1024, 896
TM_CONV = 512
TM_GLA_OUT = 512
TM_FFN, TF_FFN = 512, 512


def _params(sem):
    return pltpu.CompilerParams(dimension_semantics=sem, vmem_limit_bytes=VMEM_LIMIT)


def _rms_rows(x):
    return x * lax.rsqrt(jnp.mean(x * x, axis=-1, keepdims=True) + EPS)


def _pre_norm_mod(x, gain, scale, shift):
    return (_rms_rows(x) * gain) * (1.0 + scale) + shift


def _post_norm_residual(x, y, gain, gate):
    return x + gate * (_rms_rows(y) * gain)


def _mod_kernel(c_ref, w_ref, b_ref, o_ref):
    c = c_ref[...]
    c_act = (c * jax.nn.sigmoid(c)).astype(BF16)
    o_ref[0] = jnp.dot(c_act, w_ref[0].astype(BF16), preferred_element_type=F32) + b_ref[0]


def _modulation(c_pad, w_ada, b_ada):
    tn = TN_MOD
    depth, d, n = w_ada.shape
    rows = c_pad.shape[0]
    return pl.pallas_call(
        _mod_kernel,
        grid=(depth, n // tn),
        in_specs=[
            pl.BlockSpec((rows, d), lambda i, j: (0, 0)),
            pl.BlockSpec((1, d, tn), lambda i, j: (i, 0, j)),
            pl.BlockSpec((1, 1, tn), lambda i, j: (i, 0, j)),
        ],
        out_specs=pl.BlockSpec((1, rows, tn), lambda i, j: (i, 0, j)),
        out_shape=jax.ShapeDtypeStruct((depth, rows, n), F32),
        compiler_params=_params(("parallel", "parallel")),
        name="modulation",
    )(c_pad, w_ada, b_ada.reshape(depth, 1, n))


def _prenorm_to_scratch(x_ref, g_ref, sc_ref, sh_ref, h_ref):
    @pl.when(pl.program_id(1) == 0)
    def _():
        h = _pre_norm_mod(x_ref[...], g_ref[...], sc_ref[0], sh_ref[0])
        h_ref[...] = h.astype(BF16)


def _pm_glu_kernel(x_ref, g_ref, sc_ref, sh_ref, wa_ref, wg_ref, ba_ref, bg_ref, o_ref, h_ref):
    _prenorm_to_scratch(x_ref, g_ref, sc_ref, sh_ref, h_ref)
    h = h_ref[...]
    a = jnp.dot(h, wa_ref[...], preferred_element_type=F32) + ba_ref[...]
    g = jnp.dot(h, wg_ref[...], preferred_element_type=F32) + bg_ref[...]
    o_ref[...] = (a * jax.nn.sigmoid(g)).astype(o_ref.dtype)


def _pm_plain_kernel(x_ref, g_ref, sc_ref, sh_ref, w_ref, o_ref, h_ref):
    _prenorm_to_scratch(x_ref, g_ref, sc_ref, sh_ref, h_ref)
    o_ref[...] = jnp.dot(h_ref[...], w_ref[...], preferred_element_type=F32).astype(o_ref.dtype)


def _row_specs(tm, d, tiles_per_batch):
    return [
        pl.BlockSpec((tm, d), lambda i, j: (i, 0)),
        pl.BlockSpec((1, d), lambda i, j: (0, 0)),
        pl.BlockSpec((1, 1, d), lambda i, j: (i // tiles_per_batch, 0, 0)),
        pl.BlockSpec((1, 1, d), lambda i, j: (i // tiles_per_batch, 0, 0)),
    ]


def _prenorm_glu(x, gain, scale, shift, w, layer, bias, seq):
    tm, tn = TM_PW1, TN_PW1
    m, d = x.shape
    n = w.shape[2] // 2
    nb = n // tn
    return pl.pallas_call(
        _pm_glu_kernel,
        grid=(m // tm, nb),
        in_specs=_row_specs(tm, d, seq // tm) + [
            pl.BlockSpec((None, d, tn), lambda i, j: (layer, 0, j)),
            pl.BlockSpec((None, d, tn), lambda i, j: (layer, 0, j + nb)),
            pl.BlockSpec((1, tn), lambda i, j: (0, j)),
            pl.BlockSpec((1, tn), lambda i, j: (0, j + nb)),
        ],
        out_specs=pl.BlockSpec((tm, tn), lambda i, j: (i, j)),
        out_shape=jax.ShapeDtypeStruct((m, n), F32),
        scratch_shapes=[pltpu.VMEM((tm, d), BF16)],
        compiler_params=_params(("parallel", "arbitrary")),
        name="prenorm_pw1_glu",
    )(x, gain, scale, shift, w, w, bias, bias)


def _prenorm_matmul(x, gain, scale, shift, w, layer, seq):
    tm, tn = TM_GLA_IN, TN_GLA_IN
    m, d = x.shape
    n = w.shape[2]
    return pl.pallas_call(
        _pm_plain_kernel,
        grid=(m // tm, n // tn),
        in_specs=_row_specs(tm, d, seq // tm) + [pl.BlockSpec((None, d, tn), lambda i, j: (layer, 0, j))],
        out_specs=pl.BlockSpec((tm, tn), lambda i, j: (i, j)),
        out_shape=jax.ShapeDtypeStruct((m, n), BF16),
        scratch_shapes=[pltpu.VMEM((tm, d), BF16)],
        compiler_params=_params(("parallel", "arbitrary")),
        name="prenorm_gla_in",
    )(x, gain, scale, shift, w)


CONV_ROWS = 64
CONV_COLS = 256


def _conv_tail_kernel(u_ref, halo_ref, x_ref, wdw_ref, bdw_ref, lng_ref, lnb_ref, w2_ref, b2_ref,
                      pg_ref, gt_ref, o_ref, ext_ref, cv_ref, *, tiles_per_batch):
    tm, d = u_ref.shape
    first = (pl.program_id(0) % tiles_per_batch) == 0
    ext_ref[0:HALO, :] = jnp.where(first, 0.0, halo_ref[...])
    ext_ref[HALO:, :] = u_ref[...]

    base = HALO - (CONV_WIDTH - 1)
    for c in range(d // CONV_COLS):
        cols = slice(c * CONV_COLS, (c + 1) * CONV_COLS)

        def row_body(r, carry, cols=cols):
            r0 = pl.multiple_of(r * CONV_ROWS, CONV_ROWS)
            acc = jnp.broadcast_to(bdw_ref[:, cols], (CONV_ROWS, CONV_COLS))
            win = ext_ref[pl.ds(r0, CONV_ROWS + HALO), cols]
            tiles = [win[j:j + SUBLANES] for j in range(0, CONV_ROWS + HALO, SUBLANES)]
            sub = lax.broadcasted_iota(jnp.int32, (SUBLANES, CONV_COLS), 0)
            for res in range(SUBLANES):
                if res == 0:
                    shifted = win
                else:
                    rot = [pltpu.roll(tile, SUBLANES - res, 0) for tile in tiles]
                    shifted = jnp.concatenate([jnp.where(sub < SUBLANES - res, lo, hi)
                                               for lo, hi in zip(rot[:-1], rot[1:])], axis=0)
                for off in range(res, HALO + 1, SUBLANES):
                    k = off - base
                    if 0 <= k < CONV_WIDTH:
                        w_tile = wdw_ref[k * SUBLANES:(k + 1) * SUBLANES, cols]
                        w_rows = jnp.concatenate([w_tile] * (CONV_ROWS // SUBLANES), axis=0)
                        acc = acc + w_rows * shifted[off - res:off - res + CONV_ROWS]
            cv_ref[pl.ds(r0, CONV_ROWS), cols] = acc
            return carry

        lax.fori_loop(0, tm // CONV_ROWS, row_body, 0)

    cv = cv_ref[...]
    mu = jnp.mean(cv, axis=-1, keepdims=True)
    xc = cv - mu
    ln = xc * lax.rsqrt(jnp.mean(xc * xc, axis=-1, keepdims=True) + EPS) * lng_ref[...] + lnb_ref[...]
    v = (ln * jax.nn.sigmoid(ln)).astype(BF16)
    y = jnp.dot(v, w2_ref[...], preferred_element_type=F32) + b2_ref[...]
    o_ref[...] = _post_norm_residual(x_ref[...], y, pg_ref[...], gt_ref[0])


def _conv_tail(u, x, w_dw, b_dw, ln_g, ln_b, w2, layer, b2, post_g, gate, seq):
    tm = TM_CONV
    m, d = u.shape
    tiles_per_batch = seq // tm
    halo_blocks = tm // HALO
    vec = pl.BlockSpec((1, d), lambda i: (0, 0))
    return pl.pallas_call(
        functools.partial(_conv_tail_kernel, tiles_per_batch=tiles_per_batch),
        grid=(m // tm,),
        in_specs=[
            pl.BlockSpec((tm, d), lambda i: (i, 0)),
            pl.BlockSpec((HALO, d), lambda i: (jnp.maximum(i * halo_blocks - 1, 0), 0)),
            pl.BlockSpec((tm, d), lambda i: (i, 0)),
            pl.BlockSpec(w_dw.shape, lambda i: (0, 0)),
            vec, vec, vec,
            pl.BlockSpec((None,) + w2.shape[1:], lambda i: (layer, 0, 0)),
            vec, vec,
            pl.BlockSpec((1, 1, d), lambda i: (i // tiles_per_batch, 0, 0)),
        ],
        out_specs=pl.BlockSpec((tm, d), lambda i: (i, 0)),
        out_shape=jax.ShapeDtypeStruct((m, d), F32),
        scratch_shapes=[pltpu.VMEM((tm + HALO, d), F32), pltpu.VMEM((tm, d), F32)],
        compiler_params=_params(("arbitrary",)),
        name="conv_tail",
    )(u, u, x, w_dw, b_dw, ln_g, ln_b, w2, b2, post_g, gate)


def _gla_tables(c):
    i = np.arange(c)[:, None]
    m = np.arange(c)[None, :]
    prefix = [m <= i]
    for h in GLA_SMALL:
        ref = (i // (2 * h)) * (2 * h) + h
        upper = (i % (2 * h)) >= h
        prefix.append(np.where(upper, (m > ref) & (m <= i), (m > i) & (m <= ref)))
    prefix = np.concatenate(prefix, axis=0).astype(np.float32)
    masks = []
    h = c // 2
    while h >= 1:
        same_block = (i // (2 * h)) == (m // (2 * h))
        masks.append(same_block & ((i % (2 * h)) >= h) & ((m % (2 * h)) < h))
        h //= 2
    masks.append(i == m)
    return (jnp.asarray(np.concatenate([prefix] * 3, axis=1), BF16),
            jnp.asarray(np.stack(masks).astype(np.float32)))


def _dot_nt(x, y):
    return lax.dot_general(x, y, (((1,), (1,)), ((), ())), preferred_element_type=F32)


def _gla_kernel(q_ref, k_ref, v_ref, r_ref, a_ref, wg_ref, bg_ref, ng_ref, prefix_ref, masks_ref, o_ref, s_ref,
                *, scale):
    t, dk = q_ref.shape
    c = GLA_CHUNK
    chunks = [slice(n * c, (n + 1) * c) for n in range(t // c)]

    @pl.when(pl.program_id(2) == 0)
    def _():
        s_ref[...] = jnp.zeros_like(s_ref)

    q = q_ref[...].astype(F32) * scale
    k = k_ref[...].astype(F32)

    x = jnp.dot(a_ref[...], wg_ref[...], preferred_element_type=F32) + bg_ref[...]
    g = (jnp.minimum(x, 0.0) - jnp.log(1.0 + jnp.exp(-jnp.abs(x)))) * (1.0 / GLA_GATE_TAU)

    g_hi = g.astype(BF16)
    rest = g - g_hi.astype(F32)
    g_mid = rest.astype(BF16)
    g_lo = (rest - g_mid.astype(F32)).astype(BF16)
    sums = [jnp.dot(prefix_ref[...], jnp.concatenate([g_hi[rows], g_mid[rows], g_lo[rows]], axis=0),
                    preferred_element_type=F32) for rows in chunks]
    b = jnp.concatenate([sm[:c] for sm in sums], axis=0)

    zs = []
    h = c // 2
    while h >= SUBLANES:
        exps, rows = [], []
        for p in range(t // (2 * h)):
            lower = slice(2 * h * p, 2 * h * p + h)
            upper = slice(2 * h * p + h, 2 * h * (p + 1))
            ref = b[2 * h * p + h:2 * h * p + h + 1]
            exps += [ref - b[lower], b[upper] - ref]
            rows += [k[lower], q[upper]]
        zs.append((jnp.concatenate(rows, axis=0) * jnp.exp(jnp.concatenate(exps, axis=0))).astype(BF16))
        h //= 2
    row = lax.broadcasted_iota(jnp.int32, (t, dk), 0)
    for n, h in enumerate(GLA_SMALL):
        e = jnp.concatenate([sm[(n + 1) * c:(n + 2) * c] for sm in sums], axis=0)
        zs.append((jnp.where((row & h) != 0, q, k) * jnp.exp(e)).astype(BF16))
    q_bf, k_bf = q.astype(BF16), k.astype(BF16)
    scores = []
    for rows in chunks:
        sc = masks_ref[len(zs)] * _dot_nt(q_bf[rows], k_bf[rows])
        for n, z in enumerate(zs):
            sc = sc + masks_ref[n] * _dot_nt(z[rows], z[rows])
        scores.append(sc.astype(BF16))

    qe = (q * jnp.exp(b)).astype(BF16)
    b_last = [b[rows][c - 1:c] for rows in chunks]
    ke = k * jnp.exp(jnp.concatenate([bl - b[rows] for bl, rows in zip(b_last, chunks)], axis=0))
    r = r_ref[...].astype(F32)
    out_gate = ng_ref[...] * (r * jax.nn.sigmoid(r))

    s = s_ref[...]
    for n, rows in enumerate(chunks):
        v = v_ref[rows, :]
        o = jnp.dot(scores[n], v, preferred_element_type=F32)
        o = o + jnp.dot(qe[rows], s.astype(BF16), preferred_element_type=F32)
        ke_t = jnp.transpose(ke[rows]).astype(BF16)
        decay = jnp.transpose(jnp.broadcast_to(jnp.exp(b_last[n]), (LANES, dk)))[:, 0:1]
        s = decay * s + jnp.dot(ke_t, v, preferred_element_type=F32)
        o_ref[rows, :] = (_rms_rows(o) * out_gate[rows]).astype(o_ref.dtype)
    s_ref[...] = s


def _gla_attention(proj, w_gate, layer, b_gate, norm_g, bsz, dk, dv):
    m = proj.shape[0]
    hk, hv = dk // GLA_HEADS, dv // GLA_HEADS
    c = GLA_STEP
    nt = m // bsz // c
    prefix, masks = _gla_tables(GLA_CHUNK)
    k_off, v_off, r_off, a_off = dk // hk, 2 * dk // hv, (2 * dk + dv) // hv, (2 * dk + 2 * dv) // LANES
    return pl.pallas_call(
        functools.partial(_gla_kernel, scale=hk ** -0.5),
        grid=(bsz, GLA_HEADS, nt),
        in_specs=[
            pl.BlockSpec((c, hk), lambda b, h, t: (b * nt + t, h)),
            pl.BlockSpec((c, hk), lambda b, h, t: (b * nt + t, k_off + h)),
            pl.BlockSpec((c, hv), lambda b, h, t: (b * nt + t, v_off + h)),
            pl.BlockSpec((c, hv), lambda b, h, t: (b * nt + t, r_off + h)),
            pl.BlockSpec((c, LANES), lambda b, h, t: (b * nt + t, a_off)),
            pl.BlockSpec((None, LANES, hk), lambda b, h, t: (layer, 0, h)),
            pl.BlockSpec((1, hk), lambda b, h, t: (0, h)),
            pl.BlockSpec((1, hv), lambda b, h, t: (0, h)),
            pl.BlockSpec(prefix.shape, lambda b, h, t: (0, 0)),
            pl.BlockSpec(masks.shape, lambda b, h, t: (0, 0, 0)),
        ],
        out_specs=pl.BlockSpec((c, hv), lambda b, h, t: (b * nt + t, h)),
        out_shape=jax.ShapeDtypeStruct((m, dv), BF16),
        scratch_shapes=[pltpu.VMEM((hk, hv), F32)],
        compiler_params=_params(("parallel", "parallel", "arbitrary")),
        name="gla_attention",
    )(proj, proj, proj, proj, proj, w_gate, b_gate, norm_g, prefix, masks)


def _mm_post_kernel(a_ref, w_ref, x_ref, pg_ref, gt_ref, o_ref):
    y = jnp.dot(a_ref[...], w_ref[...], preferred_element_type=F32)
    o_ref[...] = _post_norm_residual(x_ref[...], y, pg_ref[...], gt_ref[0])


def _matmul_post(a, w, layer, x, post_g, gate, seq):
    tm = TM_GLA_OUT
    m, kdim = a.shape
    d = w.shape[2]
    tiles_per_batch = seq // tm
    return pl.pallas_call(
        _mm_post_kernel,
        grid=(m // tm,),
        in_specs=[
            pl.BlockSpec((tm, kdim), lambda i: (i, 0)),
            pl.BlockSpec((None, kdim, d), lambda i: (layer, 0, 0)),
            pl.BlockSpec((tm, d), lambda i: (i, 0)),
            pl.BlockSpec((1, d), lambda i: (0, 0)),
            pl.BlockSpec((1, 1, d), lambda i: (i // tiles_per_batch, 0, 0)),
        ],
        out_specs=pl.BlockSpec((tm, d), lambda i: (i, 0)),
        out_shape=jax.ShapeDtypeStruct((m, d), F32),
        compiler_params=_params(("parallel",)),
        name="gla_out_post",
    )(a, w, x, post_g, gate)


def _ffn_kernel(x_ref, g_ref, sc_ref, sh_ref, wg_ref, wu_ref, wo_ref, pg_ref, gt_ref, o_ref, h_ref, acc_ref):
    f = pl.program_id(1)
    _prenorm_to_scratch(x_ref, g_ref, sc_ref, sh_ref, h_ref)

    @pl.when(f == 0)
    def _():
        acc_ref[...] = jnp.zeros_like(acc_ref)

    h = h_ref[...]
    gate = jnp.dot(h, wg_ref[...], preferred_element_type=F32)
    up = jnp.dot(h, wu_ref[...], preferred_element_type=F32)
    act = (gate * jax.nn.sigmoid(gate) * up).astype(BF16)
    acc_ref[...] += jnp.dot(act, wo_ref[...], preferred_element_type=F32)

    @pl.when(f == pl.num_programs(1) - 1)
    def _():
        o_ref[...] = _post_norm_residual(x_ref[...], acc_ref[...], pg_ref[...], gt_ref[0])


def _ffn(x, gain, scale, shift, w_in, w_out, layer, post_g, gate, seq):
    tm, tf = TM_FFN, TF_FFN
    m, d = x.shape
    dff = w_out.shape[1]
    nf = dff // tf
    tiles_per_batch = seq // tm
    return pl.pallas_call(
        _ffn_kernel,
        grid=(m // tm, nf),
        in_specs=_row_specs(tm, d, tiles_per_batch) + [
            pl.BlockSpec((None, d, tf), lambda i, f: (layer, 0, f)),
            pl.BlockSpec((None, d, tf), lambda i, f: (layer, 0, f + nf)),
            pl.BlockSpec((None, tf, d), lambda i, f: (layer, f, 0)),
            pl.BlockSpec((1, d), lambda i, f: (0, 0)),
            pl.BlockSpec((1, 1, d), lambda i, f: (i // tiles_per_batch, 0, 0)),
        ],
        out_specs=pl.BlockSpec((tm, d), lambda i, f: (i, 0)),
        out_shape=jax.ShapeDtypeStruct((m, d), F32),
        scratch_shapes=[pltpu.VMEM((tm, d), BF16), pltpu.VMEM((tm, d), F32)],
        compiler_params=_params(("parallel", "arbitrary")),
        name="ffn",
    )(x, gain, scale, shift, w_in, w_in, w_out, post_g, gate)


def kernel(x, c, w_ada, b_ada, pre_mix_g, post_mix_g, pre_ffn_g, post_ffn_g, conv_w_pw1, conv_b_pw1, conv_w_dw, conv_b_dw, conv_ln_g, conv_ln_b, conv_w_pw2, conv_b_pw2, gla_w_in, gla_w_gate_up, gla_b_gate, gla_norm_g, gla_w_out, ffn_w_in, ffn_w_out):
    bsz, seq, d = x.shape
    depth = w_ada.shape[0]
    dk = gla_w_gate_up.shape[2]
    dv = gla_norm_g.shape[1]
    rank = gla_w_gate_up.shape[1]

    c_pad = jnp.zeros((8, d), F32).at[:bsz].set(c)
    mod = _modulation(c_pad, w_ada, b_ada)[:, :bsz]
    mod = mod.reshape(depth, bsz, N_MOD, 1, d)

    def row(vec):
        return vec.reshape(1, -1)

    pw1_w, pw2_w = conv_w_pw1.astype(BF16), conv_w_pw2.astype(BF16)
    gla_in_w = jnp.pad(gla_w_in, ((0, 0), (0, 0), (0, LANES - rank))).astype(BF16)
    gla_gate_w = jnp.pad(gla_w_gate_up, ((0, 0), (0, LANES - rank), (0, 0))).astype(BF16)
    gla_out_w = gla_w_out.astype(BF16)
    ffn_in_w, ffn_out_w = ffn_w_in.astype(BF16), ffn_w_out.astype(BF16)

    xf = x.reshape(bsz * seq, d)
    for i in range(depth):
        sh1, sc1, gt1, sh2, sc2, gt2 = (mod[i, :, n] for n in range(N_MOD))
        j = i // 2
        if i % 2 == 0:
            u = _prenorm_glu(xf, row(pre_mix_g[i]), sc1, sh1, pw1_w, j, row(conv_b_pw1[j]), seq)
            w_dw = jnp.repeat(conv_w_dw[j], SUBLANES, axis=0)
            xf = _conv_tail(u, xf, w_dw, row(conv_b_dw[j]), row(conv_ln_g[j]), row(conv_ln_b[j]),
                            pw2_w, j, row(conv_b_pw2[j]), row(post_mix_g[i]), gt1, seq)
        else:
            proj = _prenorm_matmul(xf, row(pre_mix_g[i]), sc1, sh1, gla_in_w, j, seq)
            o = _gla_attention(proj, gla_gate_w, j, row(gla_b_gate[j]), row(gla_norm_g[j]), bsz, dk, dv)
            xf = _matmul_post(o, gla_out_w, j, xf, row(post_mix_g[i]), gt1, seq)
        xf = _ffn(xf, row(pre_ffn_g[i]), sc2, sh2, ffn_in_w, ffn_out_w, i, row(post_ffn_g[i]), gt2, seq)
    return xf.reshape(bsz, seq, d)
```

```python
import functools

import numpy as np
import jax
import jax.numpy as jnp
from jax import lax
from jax.experimental import pallas as pl
from jax.experimental.pallas import tpu as pltpu

F32 = jnp.float32
BF16 = jnp.bfloat16

EPS = 1e-6
CONV_WIDTH = 31
GLA_HEADS = 4
GLA_GATE_RANK = 16
GLA_GATE_TAU = 16.0
N_MOD = 6

LANES = 128
HALO = 32
GLA_CHUNK = 128
GLA_STEP = 512
SUBLANES = 8
ROW_GROUP = 16
ROW_UNROLL = 8
GLA_SMALL = (4, 2, 1)
VMEM_LIMIT = 56 * 1024 * 1024

TN_MOD = 1024
TM_PW1, TN_PW1 = 1024, 512
TM_GLA_IN, TN_GLA_IN = 1024, 1280
TM_CONV = 512
TM_GLA_OUT = 512
TM_FFN, TF_FFN = 512, 512
MM_ROWS = 256


def _params(sem):
    return pltpu.CompilerParams(dimension_semantics=sem, vmem_limit_bytes=VMEM_LIMIT)


def _rms_rows(x):
    return x * lax.rsqrt(jnp.mean(x * x, axis=-1, keepdims=True) + EPS)


def _pre_norm_mod(x, gain, scale, shift):
    return (_rms_rows(x) * gain) * (1.0 + scale) + shift


def _post_norm_residual(x, y, gain, gate):
    return x + gate * (_rms_rows(y) * gain)


def _mod_kernel(c_ref, w_ref, b_ref, o_ref):
    c = c_ref[...]
    c_act = (c * jax.nn.sigmoid(c)).astype(BF16)
    o_ref[0] = jnp.dot(c_act, w_ref[0].astype(BF16), preferred_element_type=F32) + b_ref[0]


def _modulation(c_pad, w_ada, b_ada):
    tn = TN_MOD
    depth, d, n = w_ada.shape
    rows = c_pad.shape[0]
    return pl.pallas_call(
        _mod_kernel,
        grid=(depth, n // tn),
        in_specs=[
            pl.BlockSpec((rows, d), lambda i, j: (0, 0)),
            pl.BlockSpec((1, d, tn), lambda i, j: (i, 0, j)),
            pl.BlockSpec((1, 1, tn), lambda i, j: (i, 0, j)),
        ],
        out_specs=pl.BlockSpec((1, rows, tn), lambda i, j: (i, 0, j)),
        out_shape=jax.ShapeDtypeStruct((depth, rows, n), F32),
        compiler_params=_params(("parallel", "parallel")),
        name="modulation",
    )(c_pad, w_ada, b_ada.reshape(depth, 1, n))


def _row_groups(n_rows, body):
    def step(r, carry):
        body(pl.ds(pl.multiple_of(r * ROW_GROUP, ROW_GROUP), ROW_GROUP))
        return carry
    lax.fori_loop(0, n_rows // ROW_GROUP, step, 0, unroll=ROW_UNROLL)


def _group_rows(vec):
    return jnp.broadcast_to(vec, (ROW_GROUP, vec.shape[1]))


def _prenorm_to_scratch(x_ref, g_ref, sc_ref, sh_ref, h_ref):
    @pl.when(pl.program_id(1) == 0)
    def _():
        gain = _group_rows(g_ref[...] * (1.0 + sc_ref[0]))
        shift = _group_rows(sh_ref[0])

        def body(rows):
            h_ref[rows, :] = (_rms_rows(x_ref[rows, :]) * gain + shift).astype(BF16)

        _row_groups(x_ref.shape[0], body)


def _post_norm_rows(y_ref, bias, x_ref, pg_ref, gt_ref, o_ref):
    gain = _group_rows(gt_ref[0] * pg_ref[...])
    bias = None if bias is None else _group_rows(bias)

    def body(rows):
        y = y_ref[rows, :] if bias is None else y_ref[rows, :] + bias
        o_ref[rows, :] = x_ref[rows, :] + _rms_rows(y) * gain

    _row_groups(y_ref.shape[0], body)


def _pm_glu_kernel(x_ref, g_ref, sc_ref, sh_ref, wa_ref, wg_ref, ba_ref, bg_ref, o_ref, h_ref):
    _prenorm_to_scratch(x_ref, g_ref, sc_ref, sh_ref, h_ref)
    for r0 in range(0, h_ref.shape[0], MM_ROWS):
        rows = slice(r0, r0 + MM_ROWS)
        h = h_ref[rows, :]
        a = jnp.dot(h, wa_ref[...], preferred_element_type=F32) + ba_ref[...]
        g = jnp.dot(h, wg_ref[...], preferred_element_type=F32) + bg_ref[...]
        o_ref[rows, :] = (a * jax.nn.sigmoid(g)).astype(o_ref.dtype)


def _pm_plain_kernel(x_ref, g_ref, sc_ref, sh_ref, w_ref, o_ref, h_ref):
    _prenorm_to_scratch(x_ref, g_ref, sc_ref, sh_ref, h_ref)
    for r0 in range(0, h_ref.shape[0], MM_ROWS):
        rows = slice(r0, r0 + MM_ROWS)
        o_ref[rows, :] = jnp.dot(h_ref[rows, :], w_ref[...], preferred_element_type=F32).astype(o_ref.dtype)


def _row_specs(tm, d, tiles_per_batch):
    return [
        pl.BlockSpec((tm, d), lambda i, j: (i, 0)),
        pl.BlockSpec((1, d), lambda i, j: (0, 0)),
        pl.BlockSpec((1, 1, d), lambda i, j: (i // tiles_per_batch, 0, 0)),
        pl.BlockSpec((1, 1, d), lambda i, j: (i // tiles_per_batch, 0, 0)),
    ]


def _prenorm_glu(x, gain, scale, shift, w, layer, bias, seq):
    tm, tn = TM_PW1, TN_PW1
    m, d = x.shape
    n = w.shape[2] // 2
    nb = n // tn
    return pl.pallas_call(
        _pm_glu_kernel,
        grid=(m // tm, nb),
        in_specs=_row_specs(tm, d, seq // tm) + [
            pl.BlockSpec((None, d, tn), lambda i, j: (layer, 0, j)),
            pl.BlockSpec((None, d, tn), lambda i, j: (layer, 0, j + nb)),
            pl.BlockSpec((1, tn), lambda i, j: (0, j)),
            pl.BlockSpec((1, tn), lambda i, j: (0, j + nb)),
        ],
        out_specs=pl.BlockSpec((tm, tn), lambda i, j: (i, j)),
        out_shape=jax.ShapeDtypeStruct((m, n), F32),
        scratch_shapes=[pltpu.VMEM((tm, d), BF16)],
        compiler_params=_params(("parallel", "arbitrary")),
        name="prenorm_pw1_glu",
    )(x, gain, scale, shift, w, w, bias, bias)


def _prenorm_matmul(x, gain, scale, shift, w, layer, seq):
    tm, tn = TM_GLA_IN, TN_GLA_IN
    m, d = x.shape
    n = w.shape[2]
    return pl.pallas_call(
        _pm_plain_kernel,
        grid=(m // tm, n // tn),
        in_specs=_row_specs(tm, d, seq // tm) + [pl.BlockSpec((None, d, tn), lambda i, j: (layer, 0, j))],
        out_specs=pl.BlockSpec((tm, tn), lambda i, j: (i, j)),
        out_shape=jax.ShapeDtypeStruct((m, n), BF16),
        scratch_shapes=[pltpu.VMEM((tm, d), BF16)],
        compiler_params=_params(("parallel", "arbitrary")),
        name="prenorm_gla_in",
    )(x, gain, scale, shift, w)


CONV_ROWS = 64
CONV_COLS = 256


def _conv_tail_kernel(u_ref, halo_ref, x_ref, wdw_ref, bdw_ref, lng_ref, lnb_ref, w2_ref, b2_ref,
                      pg_ref, gt_ref, o_ref, ext_ref, cv_ref, v_ref, *, tiles_per_batch):
    tm, d = u_ref.shape
    first = (pl.program_id(0) % tiles_per_batch) == 0
    ext_ref[0:HALO, :] = jnp.where(first, 0.0, halo_ref[...])
    ext_ref[HALO:, :] = u_ref[...]

    base = HALO - (CONV_WIDTH - 1)
    for c in range(d // CONV_COLS):
        cols = slice(c * CONV_COLS, (c + 1) * CONV_COLS)

        def row_body(r, carry, cols=cols):
            r0 = pl.multiple_of(r * CONV_ROWS, CONV_ROWS)
            acc = jnp.broadcast_to(bdw_ref[:, cols], (CONV_ROWS, CONV_COLS))
            win = ext_ref[pl.ds(r0, CONV_ROWS + HALO), cols]
            tiles = [win[j:j + SUBLANES] for j in range(0, CONV_ROWS + HALO, SUBLANES)]
            sub = lax.broadcasted_iota(jnp.int32, (SUBLANES, CONV_COLS), 0)
            for res in range(SUBLANES):
                if res == 0:
                    shifted = win
                else:
                    rot = [pltpu.roll(tile, SUBLANES - res, 0) for tile in tiles]
                    shifted = jnp.concatenate([jnp.where(sub < SUBLANES - res, lo, hi)
                                               for lo, hi in zip(rot[:-1], rot[1:])], axis=0)
                for off in range(res, HALO + 1, SUBLANES):
                    k = off - base
                    if 0 <= k < CONV_WIDTH:
                        w_tile = wdw_ref[k * SUBLANES:(k + 1) * SUBLANES, cols]
                        w_rows = jnp.concatenate([w_tile] * (CONV_ROWS // SUBLANES), axis=0)
                        acc = acc + w_rows * shifted[off - res:off - res + CONV_ROWS]
            cv_ref[pl.ds(r0, CONV_ROWS), cols] = acc
            return carry

        lax.fori_loop(0, tm // CONV_ROWS, row_body, 0)

    ln_gain, ln_bias = _group_rows(lng_ref[...]), _group_rows(lnb_ref[...])

    def ln_body(rows):
        cv = cv_ref[rows, :]
        xc = cv - jnp.mean(cv, axis=-1, keepdims=True)
        ln = xc * lax.rsqrt(jnp.mean(xc * xc, axis=-1, keepdims=True) + EPS) * ln_gain + ln_bias
        v_ref[rows, :] = (ln * jax.nn.sigmoid(ln)).astype(BF16)

    _row_groups(tm, ln_body)
    cv_ref[...] = jnp.dot(v_ref[...], w2_ref[...], preferred_element_type=F32)
    _post_norm_rows(cv_ref, b2_ref[...], x_ref, pg_ref, gt_ref, o_ref)


def _conv_tail(u, x, w_dw, b_dw, ln_g, ln_b, w2, layer, b2, post_g, gate, seq):
    tm = TM_CONV
    m, d = u.shape
    tiles_per_batch = seq // tm
    halo_blocks = tm // HALO
    vec = pl.BlockSpec((1, d), lambda i: (0, 0))
    return pl.pallas_call(
        functools.partial(_conv_tail_kernel, tiles_per_batch=tiles_per_batch),
        grid=(m // tm,),
        in_specs=[
            pl.BlockSpec((tm, d), lambda i: (i, 0)),
            pl.BlockSpec((HALO, d), lambda i: (jnp.maximum(i * halo_blocks - 1, 0), 0)),
            pl.BlockSpec((tm, d), lambda i: (i, 0)),
            pl.BlockSpec(w_dw.shape, lambda i: (0, 0)),
            vec, vec, vec,
            pl.BlockSpec((None,) + w2.shape[1:], lambda i: (layer, 0, 0)),
            vec, vec,
            pl.BlockSpec((1, 1, d), lambda i: (i // tiles_per_batch, 0, 0)),
        ],
        out_specs=pl.BlockSpec((tm, d), lambda i: (i, 0)),
        out_shape=jax.ShapeDtypeStruct((m, d), F32),
        scratch_shapes=[pltpu.VMEM((tm + HALO, d), F32), pltpu.VMEM((tm, d), F32), pltpu.VMEM((tm, d), BF16)],
        compiler_params=_params(("arbitrary",)),
        name="conv_tail",
    )(u, u, x, w_dw, b_dw, ln_g, ln_b, w2, b2, post_g, gate)


def _gla_tables(c):
    i = np.arange(c)[:, None]
    m = np.arange(c)[None, :]
    prefix = [m <= i]
    for h in GLA_SMALL:
        ref = (i // (2 * h)) * (2 * h) + h
        upper = (i % (2 * h)) >= h
        prefix.append(np.where(upper, (m > ref) & (m <= i), (m > i) & (m <= ref)))
    prefix = np.concatenate(prefix, axis=0).astype(np.float32)
    masks = []
    h = c // 2
    while h >= 1:
        same_block = (i // (2 * h)) == (m // (2 * h))
        masks.append(same_block & ((i % (2 * h)) >= h) & ((m % (2 * h)) < h))
        h //= 2
    masks.append(i == m)
    return (jnp.asarray(np.concatenate([prefix] * 3, axis=1), BF16),
            jnp.asarray(np.stack(masks).astype(np.float32)))


def _dot_nt(x, y):
    return lax.dot_general(x, y, (((1,), (1,)), ((), ())), preferred_element_type=F32)


def _gla_kernel(q_ref, k_ref, v_ref, r_ref, a_ref, wg_ref, bg_ref, ng_ref, prefix_ref, masks_ref, o_ref, s_ref,
                *, scale):
    t, dk = q_ref.shape
    c = GLA_CHUNK
    chunks = [slice(n * c, (n + 1) * c) for n in range(t // c)]

    @pl.when(pl.program_id(2) == 0)
    def _():
        s_ref[...] = jnp.zeros_like(s_ref)

    q = q_ref[...].astype(F32) * scale
    k = k_ref[...].astype(F32)

    x = jnp.dot(a_ref[...], wg_ref[...], preferred_element_type=F32) + bg_ref[...]
    g = (jnp.minimum(x, 0.0) - jnp.log(1.0 + jnp.exp(-jnp.abs(x)))) * (1.0 / GLA_GATE_TAU)

    g_hi = g.astype(BF16)
    rest = g - g_hi.astype(F32)
    g_mid = rest.astype(BF16)
    g_lo = (rest - g_mid.astype(F32)).astype(BF16)
    sums = [jnp.dot(prefix_ref[...], jnp.concatenate([g_hi[rows], g_mid[rows], g_lo[rows]], axis=0),
                    preferred_element_type=F32) for rows in chunks]
    b = jnp.concatenate([sm[:c] for sm in sums], axis=0)

    zs = []
    h = c // 2
    while h >= SUBLANES:
        exps, rows = [], []
        for p in range(t // (2 * h)):
            lower = slice(2 * h * p, 2 * h * p + h)
            upper = slice(2 * h * p + h, 2 * h * (p + 1))
            ref = b[2 * h * p + h:2 * h * p + h + 1]
            exps += [ref - b[lower], b[upper] - ref]
            rows += [k[lower], q[upper]]
        zs.append((jnp.concatenate(rows, axis=0) * jnp.exp(jnp.concatenate(exps, axis=0))).astype(BF16))
        h //= 2
    row = lax.broadcasted_iota(jnp.int32, (t, dk), 0)
    for n, h in enumerate(GLA_SMALL):
        e = jnp.concatenate([sm[(n + 1) * c:(n + 2) * c] for sm in sums], axis=0)
        zs.append((jnp.where((row & h) != 0, q, k) * jnp.exp(e)).astype(BF16))
    q_bf, k_bf = q.astype(BF16), k.astype(BF16)
    scores = []
    for rows in chunks:
        sc = masks_ref[len(zs)] * _dot_nt(q_bf[rows], k_bf[rows])
        for n, z in enumerate(zs):
            sc = sc + masks_ref[n] * _dot_nt(z[rows], z[rows])
        scores.append(sc.astype(BF16))

    qe = (q * jnp.exp(b)).astype(BF16)
    b_last = [b[rows][c - 1:c] for rows in chunks]
    ke = k * jnp.exp(jnp.concatenate([bl - b[rows] for bl, rows in zip(b_last, chunks)], axis=0))
    r = r_ref[...].astype(F32)
    out_gate = ng_ref[...] * (r * jax.nn.sigmoid(r))

    s = s_ref[...]
    for n, rows in enumerate(chunks):
        v = v_ref[rows, :]
        o = jnp.dot(scores[n], v, preferred_element_type=F32)
        o = o + jnp.dot(qe[rows], s.astype(BF16), preferred_element_type=F32)
        ke_t = jnp.transpose(ke[rows]).astype(BF16)
        decay = jnp.transpose(jnp.broadcast_to(jnp.exp(b_last[n]), (LANES, dk)))[:, 0:1]
        s = decay * s + jnp.dot(ke_t, v, preferred_element_type=F32)
        o_ref[rows, :] = (_rms_rows(o) * out_gate[rows]).astype(o_ref.dtype)
    s_ref[...] = s


def _gla_attention(proj, w_gate, layer, b_gate, norm_g, bsz, dk, dv):
    m = proj.shape[0]
    hk, hv = dk // GLA_HEADS, dv // GLA_HEADS
    c = GLA_STEP
    nt = m // bsz // c
    prefix, masks = _gla_tables(GLA_CHUNK)
    k_off, v_off, r_off, a_off = dk // hk, 2 * dk // hv, (2 * dk + dv) // hv, (2 * dk + 2 * dv) // LANES
    return pl.pallas_call(
        functools.partial(_gla_kernel, scale=hk ** -0.5),
        grid=(bsz, GLA_HEADS, nt),
        in_specs=[
            pl.BlockSpec((c, hk), lambda b, h, t: (b * nt + t, h)),
            pl.BlockSpec((c, hk), lambda b, h, t: (b * nt + t, k_off + h)),
            pl.BlockSpec((c, hv), lambda b, h, t: (b * nt + t, v_off + h)),
            pl.BlockSpec((c, hv), lambda b, h, t: (b * nt + t, r_off + h)),
            pl.BlockSpec((c, LANES), lambda b, h, t: (b * nt + t, a_off)),
            pl.BlockSpec((None, LANES, hk), lambda b, h, t: (layer, 0, h)),
            pl.BlockSpec((1, hk), lambda b, h, t: (0, h)),
            pl.BlockSpec((1, hv), lambda b, h, t: (0, h)),
            pl.BlockSpec(prefix.shape, lambda b, h, t: (0, 0)),
            pl.BlockSpec(masks.shape, lambda b, h, t: (0, 0, 0)),
        ],
        out_specs=pl.BlockSpec((c, hv), lambda b, h, t: (b * nt + t, h)),
        out_shape=jax.ShapeDtypeStruct((m, dv), BF16),
        scratch_shapes=[pltpu.VMEM((hk, hv), F32)],
        compiler_params=_params(("parallel", "parallel", "arbitrary")),
        name="gla_attention",
    )(proj, proj, proj, proj, proj, w_gate, b_gate, norm_g, prefix, masks)


def _mm_post_kernel(a_ref, w_ref, x_ref, pg_ref, gt_ref, o_ref, y_ref):
    y_ref[...] = jnp.dot(a_ref[...], w_ref[...], preferred_element_type=F32)
    _post_norm_rows(y_ref, None, x_ref, pg_ref, gt_ref, o_ref)


def _matmul_post(a, w, layer, x, post_g, gate, seq):
    tm = TM_GLA_OUT
    m, kdim = a.shape
    d = w.shape[2]
    tiles_per_batch = seq // tm
    return pl.pallas_call(
        _mm_post_kernel,
        grid=(m // tm,),
        in_specs=[
            pl.BlockSpec((tm, kdim), lambda i: (i, 0)),
            pl.BlockSpec((None, kdim, d), lambda i: (layer, 0, 0)),
            pl.BlockSpec((tm, d), lambda i: (i, 0)),
            pl.BlockSpec((1, d), lambda i: (0, 0)),
            pl.BlockSpec((1, 1, d), lambda i: (i // tiles_per_batch, 0, 0)),
        ],
        out_specs=pl.BlockSpec((tm, d), lambda i: (i, 0)),
        out_shape=jax.ShapeDtypeStruct((m, d), F32),
        scratch_shapes=[pltpu.VMEM((tm, d), F32)],
        compiler_params=_params(("parallel",)),
        name="gla_out_post",
    )(a, w, x, post_g, gate)


def _ffn_kernel(x_ref, g_ref, sc_ref, sh_ref, wg_ref, wu_ref, wo_ref, pg_ref, gt_ref, o_ref, h_ref, acc_ref):
    f = pl.program_id(1)
    _prenorm_to_scratch(x_ref, g_ref, sc_ref, sh_ref, h_ref)

    @pl.when(f == 0)
    def _():
        acc_ref[...] = jnp.zeros_like(acc_ref)

    h = h_ref[...]
    gate = jnp.dot(h, wg_ref[...], preferred_element_type=F32)
    up = jnp.dot(h, wu_ref[...], preferred_element_type=F32)
    act = (gate * jax.nn.sigmoid(gate) * up).astype(BF16)
    acc_ref[...] += jnp.dot(act, wo_ref[...], preferred_element_type=F32)

    @pl.when(f == pl.num_programs(1) - 1)
    def _():
        _post_norm_rows(acc_ref, None, x_ref, pg_ref, gt_ref, o_ref)


def _ffn(x, gain, scale, shift, w_in, w_out, layer, post_g, gate, seq):
    tm, tf = TM_FFN, TF_FFN
    m, d = x.shape
    dff = w_out.shape[1]
    nf = dff // tf
    tiles_per_batch = seq // tm
    return pl.pallas_call(
        _ffn_kernel,
        grid=(m // tm, nf),
        in_specs=_row_specs(tm, d, tiles_per_batch) + [
            pl.BlockSpec((None, d, tf), lambda i, f: (layer, 0, f)),
            pl.BlockSpec((None, d, tf), lambda i, f: (layer, 0, f + nf)),
            pl.BlockSpec((None, tf, d), lambda i, f: (layer, f, 0)),
            pl.BlockSpec((1, d), lambda i, f: (0, 0)),
            pl.BlockSpec((1, 1, d), lambda i, f: (i // tiles_per_batch, 0, 0)),
        ],
        out_specs=pl.BlockSpec((tm, d), lambda i, f: (i, 0)),
        out_shape=jax.ShapeDtypeStruct((m, d), F32),
        scratch_shapes=[pltpu.VMEM((tm, d), BF16), pltpu.VMEM((tm, d), F32)],
        compiler_params=_params(("parallel", "arbitrary")),
        name="ffn",
    )(x, gain, scale, shift, w_in, w_in, w_out, post_g, gate)


def kernel(x, c, w_ada, b_ada, pre_mix_g, post_mix_g, pre_ffn_g, post_ffn_g, conv_w_pw1, conv_b_pw1, conv_w_dw, conv_b_dw, conv_ln_g, conv_ln_b, conv_w_pw2, conv_b_pw2, gla_w_in, gla_w_gate_up, gla_b_gate, gla_norm_g, gla_w_out, ffn_w_in, ffn_w_out):
    bsz, seq, d = x.shape
    depth = w_ada.shape[0]
    dk = gla_w_gate_up.shape[2]
    dv = gla_norm_g.shape[1]
    rank = gla_w_gate_up.shape[1]

    c_pad = jnp.zeros((8, d), F32).at[:bsz].set(c)
    mod = _modulation(c_pad, w_ada, b_ada)[:, :bsz]
    mod = mod.reshape(depth, bsz, N_MOD, 1, d)

    def row(vec):
        return vec.reshape(1, -1)

    pw1_w, pw2_w = conv_w_pw1.astype(BF16), conv_w_pw2.astype(BF16)
    in_width = pl.cdiv(gla_w_in.shape[2], TN_GLA_IN) * TN_GLA_IN
    gla_in_w = jnp.pad(gla_w_in, ((0, 0), (0, 0), (0, in_width - gla_w_in.shape[2]))).astype(BF16)
    gla_gate_w = jnp.pad(gla_w_gate_up, ((0, 0), (0, LANES - rank), (0, 0))).astype(BF16)
    gla_out_w = gla_w_out.astype(BF16)
    ffn_in_w, ffn_out_w = ffn_w_in.astype(BF16), ffn_w_out.astype(BF16)

    xf = x.reshape(bsz * seq, d)
    for i in range(depth):
        sh1, sc1, gt1, sh2, sc2, gt2 = (mod[i, :, n] for n in range(N_MOD))
        j = i // 2
        if i % 2 == 0:
            u = _prenorm_glu(xf, row(pre_mix_g[i]), sc1, sh1, pw1_w, j, row(conv_b_pw1[j]), seq)
            w_dw = jnp.repeat(conv_w_dw[j], SUBLANES, axis=0)
            xf = _conv_tail(u, xf, w_dw, row(conv_b_dw[j]), row(conv_ln_g[j]), row(conv_ln_b[j]),
                            pw2_w, j, row(conv_b_pw2[j]), row(post_mix_g[i]), gt1, seq)
        else:
            proj = _prenorm_matmul(xf, row(pre_mix_g[i]), sc1, sh1, gla_in_w, j, seq)
            o = _gla_attention(proj, gla_gate_w, j, row(gla_b_gate[j]), row(gla_norm_g[j]), bsz, dk, dv)
            xf = _matmul_post(o, gla_out_w, j, xf, row(post_mix_g[i]), gt1, seq)
        xf = _ffn(xf, row(pre_ffn_g[i]), sc2, sh2, ffn_in_w, ffn_out_w, i, row(post_ffn_g[i]), gt2, seq)
    return xf.reshape(bsz, seq, d)
```

```python
import functools

import numpy as np
import jax
import jax.numpy as jnp
from jax import lax
from jax.experimental import pallas as pl
from jax.experimental.pallas import tpu as pltpu

F32 = jnp.float32
BF16 = jnp.bfloat16

EPS = 1e-6
CONV_WIDTH = 31
GLA_HEADS = 4
GLA_GATE_RANK = 16
GLA_GATE_TAU = 16.0
N_MOD = 6

LANES = 128
HALO = 32
GLA_CHUNK = 128
GLA_STEP = 1024
SUBLANES = 8
ROW_GROUP = 16
ROW_UNROLL = 8
GLA_SMALL = (4, 2, 1)
VMEM_LIMIT = 56 * 1024 * 1024

TN_MOD = 1024
TM_PW1, TN_PW1 = 1024, 512
TM_GLA_IN, TN_GLA_IN = 1024, 1280
TM_CONV = 512
TM_GLA_OUT = 512
TM_FFN, TF_FFN = 512, 512
MM_ROWS = 256


def _params(sem):
    return pltpu.CompilerParams(dimension_semantics=sem, vmem_limit_bytes=VMEM_LIMIT)


def _rms_rows(x):
    return x * lax.rsqrt(jnp.mean(x * x, axis=-1, keepdims=True) + EPS)


def _pre_norm_mod(x, gain, scale, shift):
    return (_rms_rows(x) * gain) * (1.0 + scale) + shift


def _post_norm_residual(x, y, gain, gate):
    return x + gate * (_rms_rows(y) * gain)


def _mod_kernel(c_ref, w_ref, b_ref, o_ref):
    c = c_ref[...]
    c_act = (c * jax.nn.sigmoid(c)).astype(BF16)
    o_ref[0] = jnp.dot(c_act, w_ref[0].astype(BF16), preferred_element_type=F32) + b_ref[0]


def _modulation(c_pad, w_ada, b_ada):
    tn = TN_MOD
    depth, d, n = w_ada.shape
    rows = c_pad.shape[0]
    return pl.pallas_call(
        _mod_kernel,
        grid=(depth, n // tn),
        in_specs=[
            pl.BlockSpec((rows, d), lambda i, j: (0, 0)),
            pl.BlockSpec((1, d, tn), lambda i, j: (i, 0, j)),
            pl.BlockSpec((1, 1, tn), lambda i, j: (i, 0, j)),
        ],
        out_specs=pl.BlockSpec((1, rows, tn), lambda i, j: (i, 0, j)),
        out_shape=jax.ShapeDtypeStruct((depth, rows, n), F32),
        compiler_params=_params(("parallel", "parallel")),
        name="modulation",
    )(c_pad, w_ada, b_ada.reshape(depth, 1, n))


def _row_groups(n_rows, body):
    def step(r, carry):
        body(pl.ds(pl.multiple_of(r * ROW_GROUP, ROW_GROUP), ROW_GROUP))
        return carry
    lax.fori_loop(0, n_rows // ROW_GROUP, step, 0, unroll=ROW_UNROLL)


def _group_rows(vec):
    return jnp.broadcast_to(vec, (ROW_GROUP, vec.shape[1]))


def _prenorm_to_scratch(x_ref, g_ref, sc_ref, sh_ref, h_ref):
    @pl.when(pl.program_id(1) == 0)
    def _():
        gain = _group_rows(g_ref[...] * (1.0 + sc_ref[0]))
        shift = _group_rows(sh_ref[0])

        def body(rows):
            h_ref[rows, :] = (_rms_rows(x_ref[rows, :]) * gain + shift).astype(BF16)

        _row_groups(x_ref.shape[0], body)


def _post_norm_rows(y_ref, x_ref, pg_ref, gt_ref, o_ref):
    gain = _group_rows(gt_ref[0] * pg_ref[...])

    def body(rows):
        o_ref[rows, :] = x_ref[rows, :] + _rms_rows(y_ref[rows, :]) * gain

    _row_groups(y_ref.shape[0], body)


CONV_ROWS = 64
CONV_COLS = 128


def _conv_tile(ext_ref, wdw_ref, bdw_ref, r0, cols):
    base = HALO - (CONV_WIDTH - 1)
    acc = jnp.broadcast_to(bdw_ref[:, cols], (CONV_ROWS, CONV_COLS))
    win = ext_ref[r0:r0 + CONV_ROWS + HALO, cols]
    tiles = [win[j:j + SUBLANES] for j in range(0, CONV_ROWS + HALO, SUBLANES)]
    sub = lax.broadcasted_iota(jnp.int32, (SUBLANES, CONV_COLS), 0)
    for res in range(SUBLANES):
        if res == 0:
            shifted = win
        else:
            rot = [pltpu.roll(tile, SUBLANES - res, 0) for tile in tiles]
            shifted = jnp.concatenate([jnp.where(sub < SUBLANES - res, lo, hi)
                                       for lo, hi in zip(rot[:-1], rot[1:])], axis=0)
        for off in range(res, HALO + 1, SUBLANES):
            k = off - base
            if 0 <= k < CONV_WIDTH:
                w_tile = wdw_ref[k * SUBLANES:(k + 1) * SUBLANES, cols]
                w_rows = jnp.concatenate([w_tile] * (CONV_ROWS // SUBLANES), axis=0)
                acc = acc + w_rows * shifted[off - res:off - res + CONV_ROWS]
    return acc


def _pm_glu_conv_kernel(x_ref, g_ref, sc_ref, sh_ref, wa_ref, wg_ref, ba_ref, bg_ref, wdw_ref, bdw_ref, o_ref,
                        h_ref, ext_ref, carry_ref, *, tiles_per_batch):
    i, j = pl.program_id(0), pl.program_id(1)
    tm, tn = o_ref.shape
    _prenorm_to_scratch(x_ref, g_ref, sc_ref, sh_ref, h_ref)
    cols = pl.ds(pl.multiple_of(j * tn, tn), tn)
    first = (i % tiles_per_batch) == 0

    @pl.when(first)
    def _():
        ext_ref[0:HALO, :] = jnp.zeros((HALO, tn), F32)

    @pl.when(jnp.logical_not(first))
    def _():
        ext_ref[0:HALO, :] = carry_ref[:, cols]

    for r0 in range(0, tm, MM_ROWS):
        h = h_ref[r0:r0 + MM_ROWS, :]
        a = jnp.dot(h, wa_ref[...], preferred_element_type=F32) + ba_ref[...]
        g = jnp.dot(h, wg_ref[...], preferred_element_type=F32) + bg_ref[...]
        ext_ref[HALO + r0:HALO + r0 + MM_ROWS, :] = a * jax.nn.sigmoid(g)
        for rr in range(r0, r0 + MM_ROWS, CONV_ROWS):
            for c0 in range(0, tn, CONV_COLS):
                o_ref[rr:rr + CONV_ROWS, c0:c0 + CONV_COLS] = _conv_tile(
                    ext_ref, wdw_ref, bdw_ref, rr, slice(c0, c0 + CONV_COLS))
    carry_ref[:, cols] = ext_ref[tm:tm + HALO, :]


def _pm_plain_kernel(x_ref, g_ref, sc_ref, sh_ref, w_ref, o_ref, h_ref):
    _prenorm_to_scratch(x_ref, g_ref, sc_ref, sh_ref, h_ref)
    for r0 in range(0, h_ref.shape[0], MM_ROWS):
        rows = slice(r0, r0 + MM_ROWS)
        o_ref[rows, :] = jnp.dot(h_ref[rows, :], w_ref[...], preferred_element_type=F32).astype(o_ref.dtype)


def _row_specs(tm, d, tiles_per_batch):
    return [
        pl.BlockSpec((tm, d), lambda i, j: (i, 0)),
        pl.BlockSpec((1, d), lambda i, j: (0, 0)),
        pl.BlockSpec((1, 1, d), lambda i, j: (i // tiles_per_batch, 0, 0)),
        pl.BlockSpec((1, 1, d), lambda i, j: (i // tiles_per_batch, 0, 0)),
    ]


def _prenorm_glu_conv(x, gain, scale, shift, w, layer, bias, w_dw, b_dw, seq):
    tm, tn = TM_PW1, TN_PW1
    m, d = x.shape
    n = w.shape[2] // 2
    nb = n // tn
    return pl.pallas_call(
        functools.partial(_pm_glu_conv_kernel, tiles_per_batch=seq // tm),
        grid=(m // tm, nb),
        in_specs=_row_specs(tm, d, seq // tm) + [
            pl.BlockSpec((None, d, tn), lambda i, j: (layer, 0, j)),
            pl.BlockSpec((None, d, tn), lambda i, j: (layer, 0, j + nb)),
            pl.BlockSpec((1, tn), lambda i, j: (0, j)),
            pl.BlockSpec((1, tn), lambda i, j: (0, j + nb)),
            pl.BlockSpec((w_dw.shape[0], tn), lambda i, j: (0, j)),
            pl.BlockSpec((1, tn), lambda i, j: (0, j)),
        ],
        out_specs=pl.BlockSpec((tm, tn), lambda i, j: (i, j)),
        out_shape=jax.ShapeDtypeStruct((m, n), F32),
        scratch_shapes=[pltpu.VMEM((tm, d), BF16), pltpu.VMEM((tm + HALO, tn), F32), pltpu.VMEM((HALO, n), F32)],
        compiler_params=_params(("arbitrary", "arbitrary")),
        name="prenorm_pw1_glu_conv",
    )(x, gain, scale, shift, w, w, bias, bias, w_dw, b_dw)


def _prenorm_matmul(x, gain, scale, shift, w, layer, seq):
    tm, tn = TM_GLA_IN, TN_GLA_IN
    m, d = x.shape
    n = w.shape[2]
    return pl.pallas_call(
        _pm_plain_kernel,
        grid=(m // tm, n // tn),
        in_specs=_row_specs(tm, d, seq // tm) + [pl.BlockSpec((None, d, tn), lambda i, j: (layer, 0, j))],
        out_specs=pl.BlockSpec((tm, tn), lambda i, j: (i, j)),
        out_shape=jax.ShapeDtypeStruct((m, n), BF16),
        scratch_shapes=[pltpu.VMEM((tm, d), BF16)],
        compiler_params=_params(("parallel", "arbitrary")),
        name="prenorm_gla_in",
    )(x, gain, scale, shift, w)


def _conv_tail_kernel(cv_ref, x_ref, lng_ref, lnb_ref, w2_ref, b2_ref, pg_ref, gt_ref, o_ref):
    cv = cv_ref[...]
    xc = cv - jnp.mean(cv, axis=-1, keepdims=True)
    ln = xc * lax.rsqrt(jnp.mean(xc * xc, axis=-1, keepdims=True) + EPS) * lng_ref[...] + lnb_ref[...]
    v = (ln * jax.nn.sigmoid(ln)).astype(BF16)
    y = jnp.dot(v, w2_ref[...], preferred_element_type=F32) + b2_ref[...]
    o_ref[...] = _post_norm_residual(x_ref[...], y, pg_ref[...], gt_ref[0])


def _conv_tail(cv, x, ln_g, ln_b, w2, layer, b2, post_g, gate, seq):
    tm = TM_CONV
    m, d = cv.shape
    tiles_per_batch = seq // tm
    vec = pl.BlockSpec((1, d), lambda i: (0, 0))
    return pl.pallas_call(
        _conv_tail_kernel,
        grid=(m // tm,),
        in_specs=[
            pl.BlockSpec((tm, d), lambda i: (i, 0)),
            pl.BlockSpec((tm, d), lambda i: (i, 0)),
            vec, vec,
            pl.BlockSpec((None,) + w2.shape[1:], lambda i: (layer, 0, 0)),
            vec, vec,
            pl.BlockSpec((1, 1, d), lambda i: (i // tiles_per_batch, 0, 0)),
        ],
        out_specs=pl.BlockSpec((tm, d), lambda i: (i, 0)),
        out_shape=jax.ShapeDtypeStruct((m, d), F32),
        compiler_params=_params(("parallel",)),
        name="conv_tail",
    )(cv, x, ln_g, ln_b, w2, b2, post_g, gate)


def _gla_tables(c):
    i = np.arange(c)[:, None]
    m = np.arange(c)[None, :]
    prefix = [m <= i]
    for h in GLA_SMALL:
        ref = (i // (2 * h)) * (2 * h) + h
        upper = (i % (2 * h)) >= h
        prefix.append(np.where(upper, (m > ref) & (m <= i), (m > i) & (m <= ref)))
    prefix = np.concatenate(prefix, axis=0).astype(np.float32)
    masks = []
    h = c // 2
    while h >= 1:
        same_block = (i // (2 * h)) == (m // (2 * h))
        masks.append(same_block & ((i % (2 * h)) >= h) & ((m % (2 * h)) < h))
        h //= 2
    masks.append(i == m)
    return (jnp.asarray(np.concatenate([prefix] * 3, axis=1), BF16),
            jnp.asarray(np.stack(masks).astype(np.float32)))


def _dot_nt(x, y):
    return lax.dot_general(x, y, (((1,), (1,)), ((), ())), preferred_element_type=F32)


def _gla_kernel(q_ref, k_ref, v_ref, r_ref, a_ref, wg_ref, bg_ref, ng_ref, prefix_ref, masks_ref, o_ref, s_ref,
                *, scale):
    t, dk = q_ref.shape
    c = GLA_CHUNK
    chunks = [slice(n * c, (n + 1) * c) for n in range(t // c)]

    @pl.when(pl.program_id(2) == 0)
    def _():
        s_ref[...] = jnp.zeros_like(s_ref)

    q = q_ref[...].astype(F32) * scale
    k = k_ref[...].astype(F32)

    x = jnp.dot(a_ref[...], wg_ref[...], preferred_element_type=F32) + bg_ref[...]
    g = (jnp.minimum(x, 0.0) - jnp.log(1.0 + jnp.exp(-jnp.abs(x)))) * (1.0 / GLA_GATE_TAU)

    g_hi = g.astype(BF16)
    rest = g - g_hi.astype(F32)
    g_mid = rest.astype(BF16)
    g_lo = (rest - g_mid.astype(F32)).astype(BF16)
    sums = [jnp.dot(prefix_ref[...], jnp.concatenate([g_hi[rows], g_mid[rows], g_lo[rows]], axis=0),
                    preferred_element_type=F32) for rows in chunks]
    b = jnp.concatenate([sm[:c] for sm in sums], axis=0)

    zs = []
    h = c // 2
    while h >= SUBLANES:
        exps, rows = [], []
        for p in range(t // (2 * h)):
            lower = slice(2 * h * p, 2 * h * p + h)
            upper = slice(2 * h * p + h, 2 * h * (p + 1))
            ref = b[2 * h * p + h:2 * h * p + h + 1]
            exps += [ref - b[lower], b[upper] - ref]
            rows += [k[lower], q[upper]]
        zs.append((jnp.concatenate(rows, axis=0) * jnp.exp(jnp.concatenate(exps, axis=0))).astype(BF16))
        h //= 2
    row = lax.broadcasted_iota(jnp.int32, (t, dk), 0)
    for n, h in enumerate(GLA_SMALL):
        e = jnp.concatenate([sm[(n + 1) * c:(n + 2) * c] for sm in sums], axis=0)
        zs.append((jnp.where((row & h) != 0, q, k) * jnp.exp(e)).astype(BF16))
    q_bf, k_bf = q.astype(BF16), k.astype(BF16)
    scores = []
    for rows in chunks:
        sc = masks_ref[len(zs)] * _dot_nt(q_bf[rows], k_bf[rows])
        for n, z in enumerate(zs):
            sc = sc + masks_ref[n] * _dot_nt(z[rows], z[rows])
        scores.append(sc.astype(BF16))

    qe = (q * jnp.exp(b)).astype(BF16)
    b_last = [b[rows][c - 1:c] for rows in chunks]
    ke = k * jnp.exp(jnp.concatenate([bl - b[rows] for bl, rows in zip(b_last, chunks)], axis=0))
    r = r_ref[...].astype(F32)
    out_gate = ng_ref[...] * (r * jax.nn.sigmoid(r))

    s = s_ref[...]
    for n, rows in enumerate(chunks):
        v = v_ref[rows, :]
        o = jnp.dot(scores[n], v, preferred_element_type=F32)
        o = o + jnp.dot(qe[rows], s.astype(BF16), preferred_element_type=F32)
        ke_t = jnp.transpose(ke[rows]).astype(BF16)
        decay = jnp.transpose(jnp.broadcast_to(jnp.exp(b_last[n]), (LANES, dk)))[:, 0:1]
        s = decay * s + jnp.dot(ke_t, v, preferred_element_type=F32)
        o_ref[rows, :] = (_rms_rows(o) * out_gate[rows]).astype(o_ref.dtype)
    s_ref[...] = s


def _gla_attention(proj, w_gate, layer, b_gate, norm_g, bsz, dk, dv):
    m = proj.shape[0]
    hk, hv = dk // GLA_HEADS, dv // GLA_HEADS
    c = GLA_STEP
    nt = m // bsz // c
    prefix, masks = _gla_tables(GLA_CHUNK)
    k_off, v_off, r_off, a_off = dk // hk, 2 * dk // hv, (2 * dk + dv) // hv, (2 * dk + 2 * dv) // LANES
    return pl.pallas_call(
        functools.partial(_gla_kernel, scale=hk ** -0.5),
        grid=(bsz, GLA_HEADS, nt),
        in_specs=[
            pl.BlockSpec((c, hk), lambda b, h, t: (b * nt + t, h)),
            pl.BlockSpec((c, hk), lambda b, h, t: (b * nt + t, k_off + h)),
            pl.BlockSpec((c, hv), lambda b, h, t: (b * nt + t, v_off + h)),
            pl.BlockSpec((c, hv), lambda b, h, t: (b * nt + t, r_off + h)),
            pl.BlockSpec((c, LANES), lambda b, h, t: (b * nt + t, a_off)),
            pl.BlockSpec((None, LANES, hk), lambda b, h, t: (layer, 0, h)),
            pl.BlockSpec((1, hk), lambda b, h, t: (0, h)),
            pl.BlockSpec((1, hv), lambda b, h, t: (0, h)),
            pl.BlockSpec(prefix.shape, lambda b, h, t: (0, 0)),
            pl.BlockSpec(masks.shape, lambda b, h, t: (0, 0, 0)),
        ],
        out_specs=pl.BlockSpec((c, hv), lambda b, h, t: (b * nt + t, h)),
        out_shape=jax.ShapeDtypeStruct((m, dv), BF16),
        scratch_shapes=[pltpu.VMEM((hk, hv), F32)],
        compiler_params=_params(("parallel", "parallel", "arbitrary")),
        name="gla_attention",
    )(proj, proj, proj, proj, proj, w_gate, b_gate, norm_g, prefix, masks)


def _mm_post_kernel(a_ref, w_ref, x_ref, pg_ref, gt_ref, o_ref):
    y = jnp.dot(a_ref[...], w_ref[...], preferred_element_type=F32)
    o_ref[...] = _post_norm_residual(x_ref[...], y, pg_ref[...], gt_ref[0])


def _matmul_post(a, w, layer, x, post_g, gate, seq):
    tm = TM_GLA_OUT
    m, kdim = a.shape
    d = w.shape[2]
    tiles_per_batch = seq // tm
    return pl.pallas_call(
        _mm_post_kernel,
        grid=(m // tm,),
        in_specs=[
            pl.BlockSpec((tm, kdim), lambda i: (i, 0)),
            pl.BlockSpec((None, kdim, d), lambda i: (layer, 0, 0)),
            pl.BlockSpec((tm, d), lambda i: (i, 0)),
            pl.BlockSpec((1, d), lambda i: (0, 0)),
            pl.BlockSpec((1, 1, d), lambda i: (i // tiles_per_batch, 0, 0)),
        ],
        out_specs=pl.BlockSpec((tm, d), lambda i: (i, 0)),
        out_shape=jax.ShapeDtypeStruct((m, d), F32),
        compiler_params=_params(("parallel",)),
        name="gla_out_post",
    )(a, w, x, post_g, gate)


def _ffn_kernel(x_ref, g_ref, sc_ref, sh_ref, wg_ref, wu_ref, wo_ref, pg_ref, gt_ref, o_ref, h_ref, acc_ref):
    f = pl.program_id(1)
    _prenorm_to_scratch(x_ref, g_ref, sc_ref, sh_ref, h_ref)

    @pl.when(f == 0)
    def _():
        acc_ref[...] = jnp.zeros_like(acc_ref)

    h = h_ref[...]
    gate = jnp.dot(h, wg_ref[...], preferred_element_type=F32)
    up = jnp.dot(h, wu_ref[...], preferred_element_type=F32)
    act = (gate * jax.nn.sigmoid(gate) * up).astype(BF16)
    acc_ref[...] += jnp.dot(act, wo_ref[...], preferred_element_type=F32)

    @pl.when(f == pl.num_programs(1) - 1)
    def _():
        _post_norm_rows(acc_ref, x_ref, pg_ref, gt_ref, o_ref)


def _ffn(x, gain, scale, shift, w_in, w_out, layer, post_g, gate, seq):
    tm, tf = TM_FFN, TF_FFN
    m, d = x.shape
    dff = w_out.shape[1]
    nf = dff // tf
    tiles_per_batch = seq // tm
    return pl.pallas_call(
        _ffn_kernel,
        grid=(m // tm, nf),
        in_specs=_row_specs(tm, d, tiles_per_batch) + [
            pl.BlockSpec((None, d, tf), lambda i, f: (layer, 0, f)),
            pl.BlockSpec((None, d, tf), lambda i, f: (layer, 0, f + nf)),
            pl.BlockSpec((None, tf, d), lambda i, f: (layer, f, 0)),
            pl.BlockSpec((1, d), lambda i, f: (0, 0)),
            pl.BlockSpec((1, 1, d), lambda i, f: (i // tiles_per_batch, 0, 0)),
        ],
        out_specs=pl.BlockSpec((tm, d), lambda i, f: (i, 0)),
        out_shape=jax.ShapeDtypeStruct((m, d), F32),
        scratch_shapes=[pltpu.VMEM((tm, d), BF16), pltpu.VMEM((tm, d), F32)],
        compiler_params=_params(("parallel", "arbitrary")),
        name="ffn",
    )(x, gain, scale, shift, w_in, w_in, w_out, post_g, gate)


def kernel(x, c, w_ada, b_ada, pre_mix_g, post_mix_g, pre_ffn_g, post_ffn_g, conv_w_pw1, conv_b_pw1, conv_w_dw, conv_b_dw, conv_ln_g, conv_ln_b, conv_w_pw2, conv_b_pw2, gla_w_in, gla_w_gate_up, gla_b_gate, gla_norm_g, gla_w_out, ffn_w_in, ffn_w_out):
    bsz, seq, d = x.shape
    depth = w_ada.shape[0]
    dk = gla_w_gate_up.shape[2]
    dv = gla_norm_g.shape[1]
    rank = gla_w_gate_up.shape[1]

    c_pad = jnp.zeros((8, d), F32).at[:bsz].set(c)
    mod = _modulation(c_pad, w_ada, b_ada)[:, :bsz]
    mod = mod.reshape(depth, bsz, N_MOD, 1, d)

    def row(vec):
        return vec.reshape(1, -1)

    pw1_w, pw2_w = conv_w_pw1.astype(BF16), conv_w_pw2.astype(BF16)
    in_width = pl.cdiv(gla_w_in.shape[2], TN_GLA_IN) * TN_GLA_IN
    gla_in_w = jnp.pad(gla_w_in, ((0, 0), (0, 0), (0, in_width - gla_w_in.shape[2]))).astype(BF16)
    gla_gate_w = jnp.pad(gla_w_gate_up, ((0, 0), (0, LANES - rank), (0, 0))).astype(BF16)
    gla_out_w = gla_w_out.astype(BF16)
    ffn_in_w, ffn_out_w = ffn_w_in.astype(BF16), ffn_w_out.astype(BF16)

    xf = x.reshape(bsz * seq, d)
    for i in range(depth):
        sh1, sc1, gt1, sh2, sc2, gt2 = (mod[i, :, n] for n in range(N_MOD))
        j = i // 2
        if i % 2 == 0:
            w_dw = jnp.repeat(conv_w_dw[j], SUBLANES, axis=0)
            cv = _prenorm_glu_conv(xf, row(pre_mix_g[i]), sc1, sh1, pw1_w, j, row(conv_b_pw1[j]),
                                   w_dw, row(conv_b_dw[j]), seq)
            xf = _conv_tail(cv, xf, row(conv_ln_g[j]), row(conv_ln_b[j]),
                            pw2_w, j, row(conv_b_pw2[j]), row(post_mix_g[i]), gt1, seq)
        else:
            proj = _prenorm_matmul(xf, row(pre_mix_g[i]), sc1, sh1, gla_in_w, j, seq)
            o = _gla_attention(proj, gla_gate_w, j, row(gla_b_gate[j]), row(gla_norm_g[j]), bsz, dk, dv)
            xf = _matmul_post(o, gla_out_w, j, xf, row(post_mix_g[i]), gt1, seq)
        xf = _ffn(xf, row(pre_ffn_g[i]), sc2, sh2, ffn_in_w, ffn_out_w, i, row(post_ffn_g[i]), gt2, seq)
    return xf.reshape(bsz, seq, d)
```

```python
import functools

import numpy as np
import jax
import jax.numpy as jnp
from jax import lax
from jax.experimental import pallas as pl
from jax.experimental.pallas import tpu as pltpu

F32 = jnp.float32
BF16 = jnp.bfloat16

EPS = 1e-6
CONV_WIDTH = 31
GLA_HEADS = 4
GLA_GATE_RANK = 16
GLA_GATE_TAU = 16.0
N_MOD = 6

LANES = 128
HALO = 32
GLA_CHUNK = 128
GLA_STEP = 1024
SUBLANES = 8
ROW_GROUP = 16
ROW_UNROLL = 8
GLA_SMALL = (4, 2, 1)
VMEM_LIMIT = 56 * 1024 * 1024

TN_MOD = 2048
TM_PW1, TN_PW1 = 1024, 512
TM_GLA_IN, TN_GLA_IN = 1024, 1280
TM_CONV = 512
TM_GLA_OUT = 512
TM_FFN, TF_FFN = 512, 512
MM_ROWS = 256


def _params(sem):
    return pltpu.CompilerParams(dimension_semantics=sem, vmem_limit_bytes=VMEM_LIMIT)


def _rms_rows(x):
    return x * lax.rsqrt(jnp.mean(x * x, axis=-1, keepdims=True) + EPS)


def _pre_norm_mod(x, gain, scale, shift):
    return (_rms_rows(x) * gain) * (1.0 + scale) + shift


def _post_norm_residual(x, y, gain, gate):
    return x + gate * (_rms_rows(y) * gain)


def _mod_kernel(c_ref, w_ref, b_ref, o_ref):
    c = c_ref[...]
    c_act = (c * jax.nn.sigmoid(c)).astype(BF16)
    o_ref[0] = jnp.dot(c_act, w_ref[0].astype(BF16), preferred_element_type=F32) + b_ref[0]


def _modulation(c_pad, w_ada, b_ada):
    tn = TN_MOD
    depth, d, n = w_ada.shape
    rows = c_pad.shape[0]
    return pl.pallas_call(
        _mod_kernel,
        grid=(depth, n // tn),
        in_specs=[
            pl.BlockSpec((rows, d), lambda i, j: (0, 0)),
            pl.BlockSpec((1, d, tn), lambda i, j: (i, 0, j)),
            pl.BlockSpec((1, 1, tn), lambda i, j: (i, 0, j)),
        ],
        out_specs=pl.BlockSpec((1, rows, tn), lambda i, j: (i, 0, j)),
        out_shape=jax.ShapeDtypeStruct((depth, rows, n), F32),
        compiler_params=_params(("parallel", "parallel")),
        name="modulation",
    )(c_pad, w_ada, b_ada.reshape(depth, 1, n))


def _row_groups(n_rows, body):
    def step(r, carry):
        body(pl.ds(pl.multiple_of(r * ROW_GROUP, ROW_GROUP), ROW_GROUP))
        return carry
    lax.fori_loop(0, n_rows // ROW_GROUP, step, 0, unroll=ROW_UNROLL)


def _group_rows(vec):
    return jnp.broadcast_to(vec, (ROW_GROUP, vec.shape[1]))


def _prenorm_to_scratch(x_ref, g_ref, sc_ref, sh_ref, h_ref):
    @pl.when(pl.program_id(1) == 0)
    def _():
        gain = _group_rows(g_ref[...] * (1.0 + sc_ref[0]))
        shift = _group_rows(sh_ref[0])

        def body(rows):
            h_ref[rows, :] = (_rms_rows(x_ref[rows, :]) * gain + shift).astype(BF16)

        _row_groups(x_ref.shape[0], body)


def _post_norm_rows(y_ref, x_ref, pg_ref, gt_ref, o_ref):
    gain = _group_rows(gt_ref[0] * pg_ref[...])

    def body(rows):
        o_ref[rows, :] = x_ref[rows, :] + _rms_rows(y_ref[rows, :]) * gain

    _row_groups(y_ref.shape[0], body)


CONV_ROWS = 64
CONV_COLS = 128


def _conv_tile(ext_ref, wdw_ref, bdw_ref, r0, cols):
    base = HALO - (CONV_WIDTH - 1)
    acc = jnp.broadcast_to(bdw_ref[:, cols], (CONV_ROWS, CONV_COLS))
    win = ext_ref[r0:r0 + CONV_ROWS + HALO, cols]
    tiles = [win[j:j + SUBLANES] for j in range(0, CONV_ROWS + HALO, SUBLANES)]
    sub = lax.broadcasted_iota(jnp.int32, (SUBLANES, CONV_COLS), 0)
    for res in range(SUBLANES):
        if res == 0:
            shifted = win
        else:
            rot = [pltpu.roll(tile, SUBLANES - res, 0) for tile in tiles]
            shifted = jnp.concatenate([jnp.where(sub < SUBLANES - res, lo, hi)
                                       for lo, hi in zip(rot[:-1], rot[1:])], axis=0)
        for off in range(res, HALO + 1, SUBLANES):
            k = off - base
            if 0 <= k < CONV_WIDTH:
                w_tile = wdw_ref[k * SUBLANES:(k + 1) * SUBLANES, cols]
                w_rows = jnp.concatenate([w_tile] * (CONV_ROWS // SUBLANES), axis=0)
                acc = acc + w_rows * shifted[off - res:off - res + CONV_ROWS]
    return acc


def _pm_glu_conv_kernel(x_ref, g_ref, sc_ref, sh_ref, wa_ref, wg_ref, ba_ref, bg_ref, wdw_ref, bdw_ref, o_ref,
                        h_ref, ext_ref, carry_ref, *, tiles_per_batch):
    i, j = pl.program_id(0), pl.program_id(1)
    tm, tn = o_ref.shape
    _prenorm_to_scratch(x_ref, g_ref, sc_ref, sh_ref, h_ref)
    cols = pl.ds(pl.multiple_of(j * tn, tn), tn)
    first = (i % tiles_per_batch) == 0

    @pl.when(first)
    def _():
        ext_ref[0:HALO, :] = jnp.zeros((HALO, tn), F32)

    @pl.when(jnp.logical_not(first))
    def _():
        ext_ref[0:HALO, :] = carry_ref[:, cols]

    for r0 in range(0, tm, MM_ROWS):
        h = h_ref[r0:r0 + MM_ROWS, :]
        a = jnp.dot(h, wa_ref[...], preferred_element_type=F32) + ba_ref[...]
        g = jnp.dot(h, wg_ref[...], preferred_element_type=F32) + bg_ref[...]
        ext_ref[HALO + r0:HALO + r0 + MM_ROWS, :] = a * jax.nn.sigmoid(g)
        for rr in range(r0, r0 + MM_ROWS, CONV_ROWS):
            for c0 in range(0, tn, CONV_COLS):
                o_ref[rr:rr + CONV_ROWS, c0:c0 + CONV_COLS] = _conv_tile(
                    ext_ref, wdw_ref, bdw_ref, rr, slice(c0, c0 + CONV_COLS))
    carry_ref[:, cols] = ext_ref[tm:tm + HALO, :]


def _pm_plain_kernel(x_ref, g_ref, sc_ref, sh_ref, w_ref, o_ref, h_ref):
    _prenorm_to_scratch(x_ref, g_ref, sc_ref, sh_ref, h_ref)
    for r0 in range(0, h_ref.shape[0], MM_ROWS):
        rows = slice(r0, r0 + MM_ROWS)
        o_ref[rows, :] = jnp.dot(h_ref[rows, :], w_ref[...], preferred_element_type=F32).astype(o_ref.dtype)


def _row_specs(tm, d, tiles_per_batch):
    return [
        pl.BlockSpec((tm, d), lambda i, j: (i, 0)),
        pl.BlockSpec((1, d), lambda i, j: (0, 0)),
        pl.BlockSpec((1, 1, d), lambda i, j: (i // tiles_per_batch, 0, 0)),
        pl.BlockSpec((1, 1, d), lambda i, j: (i // tiles_per_batch, 0, 0)),
    ]


def _prenorm_glu_conv(x, gain, scale, shift, w, layer, bias, w_dw, b_dw, seq):
    tm, tn = TM_PW1, TN_PW1
    m, d = x.shape
    n = w.shape[2] // 2
    nb = n // tn
    return pl.pallas_call(
        functools.partial(_pm_glu_conv_kernel, tiles_per_batch=seq // tm),
        grid=(m // tm, nb),
        in_specs=_row_specs(tm, d, seq // tm) + [
            pl.BlockSpec((None, d, tn), lambda i, j: (layer, 0, j)),
            pl.BlockSpec((None, d, tn), lambda i, j: (layer, 0, j + nb)),
            pl.BlockSpec((1, tn), lambda i, j: (0, j)),
            pl.BlockSpec((1, tn), lambda i, j: (0, j + nb)),
            pl.BlockSpec((w_dw.shape[0], tn), lambda i, j: (0, j)),
            pl.BlockSpec((1, tn), lambda i, j: (0, j)),
        ],
        out_specs=pl.BlockSpec((tm, tn), lambda i, j: (i, j)),
        out_shape=jax.ShapeDtypeStruct((m, n), F32),
        scratch_shapes=[pltpu.VMEM((tm, d), BF16), pltpu.VMEM((tm + HALO, tn), F32), pltpu.VMEM((HALO, n), F32)],
        compiler_params=_params(("arbitrary", "arbitrary")),
        name="prenorm_pw1_glu_conv",
    )(x, gain, scale, shift, w, w, bias, bias, w_dw, b_dw)


def _prenorm_matmul(x, gain, scale, shift, w, layer, seq):
    tm, tn = TM_GLA_IN, TN_GLA_IN
    m, d = x.shape
    n = w.shape[2]
    return pl.pallas_call(
        _pm_plain_kernel,
        grid=(m // tm, n // tn),
        in_specs=_row_specs(tm, d, seq // tm) + [pl.BlockSpec((None, d, tn), lambda i, j: (layer, 0, j))],
        out_specs=pl.BlockSpec((tm, tn), lambda i, j: (i, j)),
        out_shape=jax.ShapeDtypeStruct((m, n), BF16),
        scratch_shapes=[pltpu.VMEM((tm, d), BF16)],
        compiler_params=_params(("parallel", "arbitrary")),
        name="prenorm_gla_in",
    )(x, gain, scale, shift, w)


def _conv_tail_kernel(cv_ref, x_ref, lng_ref, lnb_ref, w2_ref, b2_ref, pg_ref, gt_ref, o_ref):
    cv = cv_ref[...]
    xc = cv - jnp.mean(cv, axis=-1, keepdims=True)
    ln = xc * lax.rsqrt(jnp.mean(xc * xc, axis=-1, keepdims=True) + EPS) * lng_ref[...] + lnb_ref[...]
    v = (ln * jax.nn.sigmoid(ln)).astype(BF16)
    y = jnp.dot(v, w2_ref[...], preferred_element_type=F32) + b2_ref[...]
    o_ref[...] = _post_norm_residual(x_ref[...], y, pg_ref[...], gt_ref[0])


def _conv_tail(cv, x, ln_g, ln_b, w2, layer, b2, post_g, gate, seq):
    tm = TM_CONV
    m, d = cv.shape
    tiles_per_batch = seq // tm
    vec = pl.BlockSpec((1, d), lambda i: (0, 0))
    return pl.pallas_call(
        _conv_tail_kernel,
        grid=(m // tm,),
        in_specs=[
            pl.BlockSpec((tm, d), lambda i: (i, 0)),
            pl.BlockSpec((tm, d), lambda i: (i, 0)),
            vec, vec,
            pl.BlockSpec((None,) + w2.shape[1:], lambda i: (layer, 0, 0)),
            vec, vec,
            pl.BlockSpec((1, 1, d), lambda i: (i // tiles_per_batch, 0, 0)),
        ],
        out_specs=pl.BlockSpec((tm, d), lambda i: (i, 0)),
        out_shape=jax.ShapeDtypeStruct((m, d), F32),
        compiler_params=_params(("parallel",)),
        name="conv_tail",
    )(cv, x, ln_g, ln_b, w2, b2, post_g, gate)


def _gla_tables(c):
    i = np.arange(c)[:, None]
    m = np.arange(c)[None, :]
    prefix = [m <= i]
    for h in GLA_SMALL:
        ref = (i // (2 * h)) * (2 * h) + h
        upper = (i % (2 * h)) >= h
        prefix.append(np.where(upper, (m > ref) & (m <= i), (m > i) & (m <= ref)))
    prefix = np.concatenate(prefix, axis=0).astype(np.float32)
    masks = []
    h = c // 2
    while h >= 1:
        same_block = (i // (2 * h)) == (m // (2 * h))
        masks.append(same_block & ((i % (2 * h)) >= h) & ((m % (2 * h)) < h))
        h //= 2
    masks.append(i == m)
    return (jnp.asarray(np.concatenate([prefix] * 3, axis=1), BF16),
            jnp.asarray(np.stack(masks).astype(np.float32)))


def _dot_nt(x, y):
    return lax.dot_general(x, y, (((1,), (1,)), ((), ())), preferred_element_type=F32)


def _gla_kernel(q_ref, k_ref, v_ref, r_ref, a_ref, wg_ref, bg_ref, ng_ref, prefix_ref, masks_ref, o_ref, s_ref,
                *, scale):
    t, dk = q_ref.shape
    c = GLA_CHUNK
    chunks = [slice(n * c, (n + 1) * c) for n in range(t // c)]

    @pl.when(pl.program_id(2) == 0)
    def _():
        s_ref[...] = jnp.zeros_like(s_ref)

    q = q_ref[...].astype(F32) * scale
    k = k_ref[...].astype(F32)

    x = jnp.dot(a_ref[...], wg_ref[...], preferred_element_type=F32) + bg_ref[...]
    g = (jnp.minimum(x, 0.0) - jnp.log(1.0 + jnp.exp(-jnp.abs(x)))) * (1.0 / GLA_GATE_TAU)

    g_hi = g.astype(BF16)
    rest = g - g_hi.astype(F32)
    g_mid = rest.astype(BF16)
    g_lo = (rest - g_mid.astype(F32)).astype(BF16)
    sums = [jnp.dot(prefix_ref[...], jnp.concatenate([g_hi[rows], g_mid[rows], g_lo[rows]], axis=0),
                    preferred_element_type=F32) for rows in chunks]
    b = jnp.concatenate([sm[:c] for sm in sums], axis=0)

    zs = []
    h = c // 2
    while h >= SUBLANES:
        exps, rows = [], []
        for p in range(t // (2 * h)):
            lower = slice(2 * h * p, 2 * h * p + h)
            upper = slice(2 * h * p + h, 2 * h * (p + 1))
            ref = b[2 * h * p + h:2 * h * p + h + 1]
            exps += [ref - b[lower], b[upper] - ref]
            rows += [k[lower], q[upper]]
        zs.append((jnp.concatenate(rows, axis=0) * jnp.exp(jnp.concatenate(exps, axis=0))).astype(BF16))
        h //= 2
    row = lax.broadcasted_iota(jnp.int32, (t, dk), 0)
    for n, h in enumerate(GLA_SMALL):
        e = jnp.concatenate([sm[(n + 1) * c:(n + 2) * c] for sm in sums], axis=0)
        zs.append((jnp.where((row & h) != 0, q, k) * jnp.exp(e)).astype(BF16))
    q_bf, k_bf = q.astype(BF16), k.astype(BF16)
    scores = []
    for rows in chunks:
        sc = masks_ref[len(zs)] * _dot_nt(q_bf[rows], k_bf[rows])
        for n, z in enumerate(zs):
            sc = sc + masks_ref[n] * _dot_nt(z[rows], z[rows])
        scores.append(sc.astype(BF16))

    qe = (q * jnp.exp(b)).astype(BF16)
    b_last = [b[rows][c - 1:c] for rows in chunks]
    ke = k * jnp.exp(jnp.concatenate([bl - b[rows] for bl, rows in zip(b_last, chunks)], axis=0))
    r = r_ref[...].astype(F32)
    out_gate = ng_ref[...] * (r * jax.nn.sigmoid(r))

    s = s_ref[...]
    for n, rows in enumerate(chunks):
        v = v_ref[rows, :]
        o = jnp.dot(scores[n], v, preferred_element_type=F32)
        o = o + jnp.dot(qe[rows], s.astype(BF16), preferred_element_type=F32)
        ke_t = jnp.transpose(ke[rows]).astype(BF16)
        decay = jnp.transpose(jnp.broadcast_to(jnp.exp(b_last[n]), (LANES, dk)))[:, 0:1]
        s = decay * s + jnp.dot(ke_t, v, preferred_element_type=F32)
        o_ref[rows, :] = (_rms_rows(o) * out_gate[rows]).astype(o_ref.dtype)
    s_ref[...] = s


def _gla_attention(proj, w_gate, layer, b_gate, norm_g, bsz, dk, dv):
    m = proj.shape[0]
    hk, hv = dk // GLA_HEADS, dv // GLA_HEADS
    c = GLA_STEP
    nt = m // bsz // c
    prefix, masks = _gla_tables(GLA_CHUNK)
    k_off, v_off, r_off, a_off = dk // hk, 2 * dk // hv, (2 * dk + dv) // hv, (2 * dk + 2 * dv) // LANES
    return pl.pallas_call(
        functools.partial(_gla_kernel, scale=hk ** -0.5),
        grid=(bsz, GLA_HEADS, nt),
        in_specs=[
            pl.BlockSpec((c, hk), lambda b, h, t: (b * nt + t, h)),
            pl.BlockSpec((c, hk), lambda b, h, t: (b * nt + t, k_off + h)),
            pl.BlockSpec((c, hv), lambda b, h, t: (b * nt + t, v_off + h)),
            pl.BlockSpec((c, hv), lambda b, h, t: (b * nt + t, r_off + h)),
            pl.BlockSpec((c, LANES), lambda b, h, t: (b * nt + t, a_off)),
            pl.BlockSpec((None, LANES, hk), lambda b, h, t: (layer, 0, h)),
            pl.BlockSpec((1, hk), lambda b, h, t: (0, h)),
            pl.BlockSpec((1, hv), lambda b, h, t: (0, h)),
            pl.BlockSpec(prefix.shape, lambda b, h, t: (0, 0)),
            pl.BlockSpec(masks.shape, lambda b, h, t: (0, 0, 0)),
        ],
        out_specs=pl.BlockSpec((c, hv), lambda b, h, t: (b * nt + t, h)),
        out_shape=jax.ShapeDtypeStruct((m, dv), BF16),
        scratch_shapes=[pltpu.VMEM((hk, hv), F32)],
        compiler_params=_params(("parallel", "parallel", "arbitrary")),
        name="gla_attention",
    )(proj, proj, proj, proj, proj, w_gate, b_gate, norm_g, prefix, masks)


def _mm_post_kernel(a_ref, w_ref, x_ref, pg_ref, gt_ref, o_ref):
    y = jnp.dot(a_ref[...], w_ref[...], preferred_element_type=F32)
    o_ref[...] = _post_norm_residual(x_ref[...], y, pg_ref[...], gt_ref[0])


def _matmul_post(a, w, layer, x, post_g, gate, seq):
    tm = TM_GLA_OUT
    m, kdim = a.shape
    d = w.shape[2]
    tiles_per_batch = seq // tm
    return pl.pallas_call(
        _mm_post_kernel,
        grid=(m // tm,),
        in_specs=[
            pl.BlockSpec((tm, kdim), lambda i: (i, 0)),
            pl.BlockSpec((None, kdim, d), lambda i: (layer, 0, 0)),
            pl.BlockSpec((tm, d), lambda i: (i, 0)),
            pl.BlockSpec((1, d), lambda i: (0, 0)),
            pl.BlockSpec((1, 1, d), lambda i: (i // tiles_per_batch, 0, 0)),
        ],
        out_specs=pl.BlockSpec((tm, d), lambda i: (i, 0)),
        out_shape=jax.ShapeDtypeStruct((m, d), F32),
        compiler_params=_params(("parallel",)),
        name="gla_out_post",
    )(a, w, x, post_g, gate)


def _ffn_kernel(x_ref, g_ref, sc_ref, sh_ref, wg_ref, wu_ref, wo_ref, pg_ref, gt_ref, o_ref, h_ref, acc_ref):
    f = pl.program_id(1)
    _prenorm_to_scratch(x_ref, g_ref, sc_ref, sh_ref, h_ref)

    @pl.when(f == 0)
    def _():
        acc_ref[...] = jnp.zeros_like(acc_ref)

    chunks = [slice(r0, r0 + MM_ROWS) for r0 in range(0, h_ref.shape[0], MM_ROWS)]
    gate_up = []
    for rows in chunks:
        h = h_ref[rows, :]
        gate_up.append((jnp.dot(h, wg_ref[...], preferred_element_type=F32),
                        jnp.dot(h, wu_ref[...], preferred_element_type=F32)))
    for rows, (gate, up) in zip(chunks, gate_up):
        act = (gate * jax.nn.sigmoid(gate) * up).astype(BF16)
        acc_ref[rows, :] += jnp.dot(act, wo_ref[...], preferred_element_type=F32)

    @pl.when(f == pl.num_programs(1) - 1)
    def _():
        _post_norm_rows(acc_ref, x_ref, pg_ref, gt_ref, o_ref)


def _ffn(x, gain, scale, shift, w_in, w_out, layer, post_g, gate, seq):
    tm, tf = TM_FFN, TF_FFN
    m, d = x.shape
    dff = w_out.shape[1]
    nf = dff // tf
    tiles_per_batch = seq // tm
    return pl.pallas_call(
        _ffn_kernel,
        grid=(m // tm, nf),
        in_specs=_row_specs(tm, d, tiles_per_batch) + [
            pl.BlockSpec((None, d, tf), lambda i, f: (layer, 0, f)),
            pl.BlockSpec((None, d, tf), lambda i, f: (layer, 0, f + nf)),
            pl.BlockSpec((None, tf, d), lambda i, f: (layer, f, 0)),
            pl.BlockSpec((1, d), lambda i, f: (0, 0)),
            pl.BlockSpec((1, 1, d), lambda i, f: (i // tiles_per_batch, 0, 0)),
        ],
        out_specs=pl.BlockSpec((tm, d), lambda i, f: (i, 0)),
        out_shape=jax.ShapeDtypeStruct((m, d), F32),
        scratch_shapes=[pltpu.VMEM((tm, d), BF16), pltpu.VMEM((tm, d), F32)],
        compiler_params=_params(("parallel", "arbitrary")),
        name="ffn",
    )(x, gain, scale, shift, w_in, w_in, w_out, post_g, gate)


def kernel(x, c, w_ada, b_ada, pre_mix_g, post_mix_g, pre_ffn_g, post_ffn_g, conv_w_pw1, conv_b_pw1, conv_w_dw, conv_b_dw, conv_ln_g, conv_ln_b, conv_w_pw2, conv_b_pw2, gla_w_in, gla_w_gate_up, gla_b_gate, gla_norm_g, gla_w_out, ffn_w_in, ffn_w_out):
    bsz, seq, d = x.shape
    depth = w_ada.shape[0]
    dk = gla_w_gate_up.shape[2]
    dv = gla_norm_g.shape[1]
    rank = gla_w_gate_up.shape[1]

    c_pad = jnp.zeros((8, d), F32).at[:bsz].set(c)
    mod = _modulation(c_pad, w_ada, b_ada)[:, :bsz]
    mod = mod.reshape(depth, bsz, N_MOD, 1, d)

    def row(vec):
        return vec.reshape(1, -1)

    pw1_w, pw2_w = conv_w_pw1.astype(BF16), conv_w_pw2.astype(BF16)
    in_width = pl.cdiv(gla_w_in.shape[2], TN_GLA_IN) * TN_GLA_IN
    gla_in_w = jnp.pad(gla_w_in.astype(BF16), ((0, 0), (0, 0), (0, in_width - gla_w_in.shape[2])))
    gla_gate_w = jnp.pad(gla_w_gate_up.astype(BF16), ((0, 0), (0, LANES - rank), (0, 0)))
    gla_out_w = gla_w_out.astype(BF16)
    ffn_in_w, ffn_out_w = ffn_w_in.astype(BF16), ffn_w_out.astype(BF16)

    xf = x.reshape(bsz * seq, d)
    for i in range(depth):
        sh1, sc1, gt1, sh2, sc2, gt2 = (mod[i, :, n] for n in range(N_MOD))
        j = i // 2
        if i % 2 == 0:
            w_dw = jnp.repeat(conv_w_dw[j], SUBLANES, axis=0)
            cv = _prenorm_glu_conv(xf, row(pre_mix_g[i]), sc1, sh1, pw1_w, j, row(conv_b_pw1[j]),
                                   w_dw, row(conv_b_dw[j]), seq)
            xf = _conv_tail(cv, xf, row(conv_ln_g[j]), row(conv_ln_b[j]),
                            pw2_w, j, row(conv_b_pw2[j]), row(post_mix_g[i]), gt1, seq)
        else:
            proj = _prenorm_matmul(xf, row(pre_mix_g[i]), sc1, sh1, gla_in_w, j, seq)
            o = _gla_attention(proj, gla_gate_w, j, row(gla_b_gate[j]), row(gla_norm_g[j]), bsz, dk, dv)
            xf = _matmul_post(o, gla_out_w, j, xf, row(post_mix_g[i]), gt1, seq)
        xf = _ffn(xf, row(pre_ffn_g[i]), sc2, sh2, ffn_in_w, ffn_out_w, i, row(post_ffn_g[i]), gt2, seq)
    return xf.reshape(bsz, seq, d)
```

```python
import functools

import numpy as np
import jax
import jax.numpy as jnp
from jax import lax
from jax.experimental import pallas as pl
from jax.experimental.pallas import tpu as pltpu

F32 = jnp.float32
BF16 = jnp.bfloat16

EPS = 1e-6
CONV_WIDTH = 31
GLA_HEADS = 4
GLA_GATE_RANK = 16
GLA_GATE_TAU = 16.0
N_MOD = 6

LANES = 128
HALO = 32
GLA_CHUNK = 128
GLA_STEP = 1024
SUBLANES = 8
ROW_GROUP = 16
ROW_UNROLL = 8
GLA_SMALL = (4, 2, 1)
VMEM_LIMIT = 56 * 1024 * 1024

TN_MOD = 1024
TM_PW1, TN_PW1 = 1024, 512
TM_GLA_IN, TN_GLA_IN = 1024, 1280
TM_CONV = 512
TM_GLA_OUT = 512
TM_FFN, TF_FFN = 512, 512
TN_FFN_OUT = 512
MM_ROWS = 256


def _params(sem):
    return pltpu.CompilerParams(dimension_semantics=sem, vmem_limit_bytes=VMEM_LIMIT)


def _rms_rows(x):
    return x * lax.rsqrt(jnp.mean(x * x, axis=-1, keepdims=True) + EPS)


def _pre_norm_mod(x, gain, scale, shift):
    return (_rms_rows(x) * gain) * (1.0 + scale) + shift


def _post_norm_residual(x, y, gain, gate):
    return x + gate * (_rms_rows(y) * gain)


def _mod_kernel(c_ref, w_ref, b_ref, o_ref):
    c = c_ref[...]
    c_act = (c * jax.nn.sigmoid(c)).astype(BF16)
    o_ref[0] = jnp.dot(c_act, w_ref[0].astype(BF16), preferred_element_type=F32) + b_ref[0]


def _modulation(c_pad, w_ada, b_ada):
    tn = TN_MOD
    depth, d, n = w_ada.shape
    rows = c_pad.shape[0]
    return pl.pallas_call(
        _mod_kernel,
        grid=(depth, n // tn),
        in_specs=[
            pl.BlockSpec((rows, d), lambda i, j: (0, 0)),
            pl.BlockSpec((1, d, tn), lambda i, j: (i, 0, j)),
            pl.BlockSpec((1, 1, tn), lambda i, j: (i, 0, j)),
        ],
        out_specs=pl.BlockSpec((1, rows, tn), lambda i, j: (i, 0, j)),
        out_shape=jax.ShapeDtypeStruct((depth, rows, n), F32),
        compiler_params=_params(("parallel", "parallel")),
        name="modulation",
    )(c_pad, w_ada, b_ada.reshape(depth, 1, n))


def _row_groups(n_rows, body):
    def step(r, carry):
        body(pl.ds(pl.multiple_of(r * ROW_GROUP, ROW_GROUP), ROW_GROUP))
        return carry
    lax.fori_loop(0, n_rows // ROW_GROUP, step, 0, unroll=ROW_UNROLL)


def _group_rows(vec):
    return jnp.broadcast_to(vec, (ROW_GROUP, vec.shape[1]))


def _prenorm_to_scratch(x_ref, g_ref, sc_ref, sh_ref, h_ref):
    @pl.when(pl.program_id(1) == 0)
    def _():
        gain = _group_rows(g_ref[...] * (1.0 + sc_ref[0]))
        shift = _group_rows(sh_ref[0])

        def body(rows):
            h_ref[rows, :] = (_rms_rows(x_ref[rows, :]) * gain + shift).astype(BF16)

        _row_groups(x_ref.shape[0], body)


def _post_norm_rows(y_ref, x_ref, pg_ref, gt_ref, o_ref):
    gain = _group_rows(gt_ref[0] * pg_ref[...])

    def body(rows):
        o_ref[rows, :] = x_ref[rows, :] + _rms_rows(y_ref[rows, :]) * gain

    _row_groups(y_ref.shape[0], body)


CONV_ROWS = 64
CONV_COLS = 128


def _conv_tile(ext_ref, wdw_ref, bdw_ref, r0, cols):
    base = HALO - (CONV_WIDTH - 1)
    acc = jnp.broadcast_to(bdw_ref[:, cols], (CONV_ROWS, CONV_COLS))
    win = ext_ref[r0:r0 + CONV_ROWS + HALO, cols]
    tiles = [win[j:j + SUBLANES] for j in range(0, CONV_ROWS + HALO, SUBLANES)]
    sub = lax.broadcasted_iota(jnp.int32, (SUBLANES, CONV_COLS), 0)
    for res in range(SUBLANES):
        if res == 0:
            shifted = win
        else:
            rot = [pltpu.roll(tile, SUBLANES - res, 0) for tile in tiles]
            shifted = jnp.concatenate([jnp.where(sub < SUBLANES - res, lo, hi)
                                       for lo, hi in zip(rot[:-1], rot[1:])], axis=0)
        for off in range(res, HALO + 1, SUBLANES):
            k = off - base
            if 0 <= k < CONV_WIDTH:
                w_tile = wdw_ref[k * SUBLANES:(k + 1) * SUBLANES, cols]
                w_rows = jnp.concatenate([w_tile] * (CONV_ROWS // SUBLANES), axis=0)
                acc = acc + w_rows * shifted[off - res:off - res + CONV_ROWS]
    return acc


def _pm_glu_conv_kernel(x_ref, g_ref, sc_ref, sh_ref, wa_ref, wg_ref, ba_ref, bg_ref, wdw_ref, bdw_ref, o_ref,
                        h_ref, ext_ref, carry_ref, *, tiles_per_batch):
    i, j = pl.program_id(0), pl.program_id(1)
    tm, tn = o_ref.shape
    _prenorm_to_scratch(x_ref, g_ref, sc_ref, sh_ref, h_ref)
    cols = pl.ds(pl.multiple_of(j * tn, tn), tn)
    first = (i % tiles_per_batch) == 0

    @pl.when(first)
    def _():
        ext_ref[0:HALO, :] = jnp.zeros((HALO, tn), F32)

    @pl.when(jnp.logical_not(first))
    def _():
        ext_ref[0:HALO, :] = carry_ref[:, cols]

    for r0 in range(0, tm, MM_ROWS):
        h = h_ref[r0:r0 + MM_ROWS, :]
        a = jnp.dot(h, wa_ref[...], preferred_element_type=F32) + ba_ref[...]
        g = jnp.dot(h, wg_ref[...], preferred_element_type=F32) + bg_ref[...]
        ext_ref[HALO + r0:HALO + r0 + MM_ROWS, :] = a * jax.nn.sigmoid(g)
        for rr in range(r0, r0 + MM_ROWS, CONV_ROWS):
            for c0 in range(0, tn, CONV_COLS):
                o_ref[rr:rr + CONV_ROWS, c0:c0 + CONV_COLS] = _conv_tile(
                    ext_ref, wdw_ref, bdw_ref, rr, slice(c0, c0 + CONV_COLS))
    carry_ref[:, cols] = ext_ref[tm:tm + HALO, :]


def _pm_plain_kernel(x_ref, g_ref, sc_ref, sh_ref, w_ref, o_ref, h_ref):
    _prenorm_to_scratch(x_ref, g_ref, sc_ref, sh_ref, h_ref)
    for r0 in range(0, h_ref.shape[0], MM_ROWS):
        rows = slice(r0, r0 + MM_ROWS)
        o_ref[rows, :] = jnp.dot(h_ref[rows, :], w_ref[...], preferred_element_type=F32).astype(o_ref.dtype)


def _row_specs(tm, d, tiles_per_batch):
    return [
        pl.BlockSpec((tm, d), lambda i, j: (i, 0)),
        pl.BlockSpec((1, d), lambda i, j: (0, 0)),
        pl.BlockSpec((1, 1, d), lambda i, j: (i // tiles_per_batch, 0, 0)),
        pl.BlockSpec((1, 1, d), lambda i, j: (i // tiles_per_batch, 0, 0)),
    ]


def _prenorm_glu_conv(x, gain, scale, shift, w, layer, bias, w_dw, b_dw, seq):
    tm, tn = TM_PW1, TN_PW1
    m, d = x.shape
    n = w.shape[2] // 2
    nb = n // tn
    return pl.pallas_call(
        functools.partial(_pm_glu_conv_kernel, tiles_per_batch=seq // tm),
        grid=(m // tm, nb),
        in_specs=_row_specs(tm, d, seq // tm) + [
            pl.BlockSpec((None, d, tn), lambda i, j: (layer, 0, j)),
            pl.BlockSpec((None, d, tn), lambda i, j: (layer, 0, j + nb)),
            pl.BlockSpec((1, tn), lambda i, j: (0, j)),
            pl.BlockSpec((1, tn), lambda i, j: (0, j + nb)),
            pl.BlockSpec((w_dw.shape[0], tn), lambda i, j: (0, j)),
            pl.BlockSpec((1, tn), lambda i, j: (0, j)),
        ],
        out_specs=pl.BlockSpec((tm, tn), lambda i, j: (i, j)),
        out_shape=jax.ShapeDtypeStruct((m, n), F32),
        scratch_shapes=[pltpu.VMEM((tm, d), BF16), pltpu.VMEM((tm + HALO, tn), F32), pltpu.VMEM((HALO, n), F32)],
        compiler_params=_params(("arbitrary", "arbitrary")),
        name="prenorm_pw1_glu_conv",
    )(x, gain, scale, shift, w, w, bias, bias, w_dw, b_dw)


def _prenorm_matmul(x, gain, scale, shift, w, layer, seq):
    tm, tn = TM_GLA_IN, TN_GLA_IN
    m, d = x.shape
    n = w.shape[2]
    return pl.pallas_call(
        _pm_plain_kernel,
        grid=(m // tm, n // tn),
        in_specs=_row_specs(tm, d, seq // tm) + [pl.BlockSpec((None, d, tn), lambda i, j: (layer, 0, j))],
        out_specs=pl.BlockSpec((tm, tn), lambda i, j: (i, j)),
        out_shape=jax.ShapeDtypeStruct((m, n), BF16),
        scratch_shapes=[pltpu.VMEM((tm, d), BF16)],
        compiler_params=_params(("parallel", "arbitrary")),
        name="prenorm_gla_in",
    )(x, gain, scale, shift, w)


def _conv_tail_kernel(cv_ref, x_ref, lng_ref, lnb_ref, w2_ref, b2_ref, pg_ref, gt_ref, o_ref):
    cv = cv_ref[...]
    xc = cv - jnp.mean(cv, axis=-1, keepdims=True)
    ln = xc * lax.rsqrt(jnp.mean(xc * xc, axis=-1, keepdims=True) + EPS) * lng_ref[...] + lnb_ref[...]
    v = (ln * jax.nn.sigmoid(ln)).astype(BF16)
    y = jnp.dot(v, w2_ref[...], preferred_element_type=F32) + b2_ref[...]
    o_ref[...] = _post_norm_residual(x_ref[...], y, pg_ref[...], gt_ref[0])


def _conv_tail(cv, x, ln_g, ln_b, w2, layer, b2, post_g, gate, seq):
    tm = TM_CONV
    m, d = cv.shape
    tiles_per_batch = seq // tm
    vec = pl.BlockSpec((1, d), lambda i: (0, 0))
    return pl.pallas_call(
        _conv_tail_kernel,
        grid=(m // tm,),
        in_specs=[
            pl.BlockSpec((tm, d), lambda i: (i, 0)),
            pl.BlockSpec((tm, d), lambda i: (i, 0)),
            vec, vec,
            pl.BlockSpec((None,) + w2.shape[1:], lambda i: (layer, 0, 0)),
            vec, vec,
            pl.BlockSpec((1, 1, d), lambda i: (i // tiles_per_batch, 0, 0)),
        ],
        out_specs=pl.BlockSpec((tm, d), lambda i: (i, 0)),
        out_shape=jax.ShapeDtypeStruct((m, d), F32),
        compiler_params=_params(("parallel",)),
        name="conv_tail",
    )(cv, x, ln_g, ln_b, w2, b2, post_g, gate)


def _gla_tables(c):
    i = np.arange(c)[:, None]
    m = np.arange(c)[None, :]
    prefix = [m <= i]
    for h in GLA_SMALL:
        ref = (i // (2 * h)) * (2 * h) + h
        upper = (i % (2 * h)) >= h
        prefix.append(np.where(upper, (m > ref) & (m <= i), (m > i) & (m <= ref)))
    prefix = np.concatenate(prefix, axis=0).astype(np.float32)
    masks = []
    h = c // 2
    while h >= 1:
        same_block = (i // (2 * h)) == (m // (2 * h))
        masks.append(same_block & ((i % (2 * h)) >= h) & ((m % (2 * h)) < h))
        h //= 2
    masks.append(i == m)
    return (jnp.asarray(np.concatenate([prefix] * 3, axis=1), BF16),
            jnp.asarray(np.stack(masks).astype(np.float32)))


def _dot_nt(x, y):
    return lax.dot_general(x, y, (((1,), (1,)), ((), ())), preferred_element_type=F32)


def _gla_kernel(q_ref, k_ref, v_ref, r_ref, a_ref, wg_ref, bg_ref, ng_ref, prefix_ref, masks_ref, o_ref, s_ref,
                *, scale):
    t, dk = q_ref.shape
    c = GLA_CHUNK
    chunks = [slice(n * c, (n + 1) * c) for n in range(t // c)]

    @pl.when(pl.program_id(2) == 0)
    def _():
        s_ref[...] = jnp.zeros_like(s_ref)

    q = q_ref[...].astype(F32) * scale
    k = k_ref[...].astype(F32)

    x = jnp.dot(a_ref[...], wg_ref[...], preferred_element_type=F32) + bg_ref[...]
    g = (jnp.minimum(x, 0.0) - jnp.log(1.0 + jnp.exp(-jnp.abs(x)))) * (1.0 / GLA_GATE_TAU)

    g_hi = g.astype(BF16)
    rest = g - g_hi.astype(F32)
    g_mid = rest.astype(BF16)
    g_lo = (rest - g_mid.astype(F32)).astype(BF16)
    sums = [jnp.dot(prefix_ref[...], jnp.concatenate([g_hi[rows], g_mid[rows], g_lo[rows]], axis=0),
                    preferred_element_type=F32) for rows in chunks]
    b = jnp.concatenate([sm[:c] for sm in sums], axis=0)

    zs = []
    h = c // 2
    while h >= SUBLANES:
        exps, rows = [], []
        for p in range(t // (2 * h)):
            lower = slice(2 * h * p, 2 * h * p + h)
            upper = slice(2 * h * p + h, 2 * h * (p + 1))
            ref = b[2 * h * p + h:2 * h * p + h + 1]
            exps += [ref - b[lower], b[upper] - ref]
            rows += [k[lower], q[upper]]
        zs.append((jnp.concatenate(rows, axis=0) * jnp.exp(jnp.concatenate(exps, axis=0))).astype(BF16))
        h //= 2
    row = lax.broadcasted_iota(jnp.int32, (t, dk), 0)
    for n, h in enumerate(GLA_SMALL):
        e = jnp.concatenate([sm[(n + 1) * c:(n + 2) * c] for sm in sums], axis=0)
        zs.append((jnp.where((row & h) != 0, q, k) * jnp.exp(e)).astype(BF16))
    q_bf, k_bf = q.astype(BF16), k.astype(BF16)
    scores = []
    for rows in chunks:
        sc = masks_ref[len(zs)] * _dot_nt(q_bf[rows], k_bf[rows])
        for n, z in enumerate(zs):
            sc = sc + masks_ref[n] * _dot_nt(z[rows], z[rows])
        scores.append(sc.astype(BF16))

    qe = (q * jnp.exp(b)).astype(BF16)
    b_last = [b[rows][c - 1:c] for rows in chunks]
    ke = k * jnp.exp(jnp.concatenate([bl - b[rows] for bl, rows in zip(b_last, chunks)], axis=0))
    r = r_ref[...].astype(F32)
    out_gate = ng_ref[...] * (r * jax.nn.sigmoid(r))

    s = s_ref[...]
    for n, rows in enumerate(chunks):
        v = v_ref[rows, :]
        o = jnp.dot(scores[n], v, preferred_element_type=F32)
        o = o + jnp.dot(qe[rows], s.astype(BF16), preferred_element_type=F32)
        ke_t = jnp.transpose(ke[rows]).astype(BF16)
        decay = jnp.transpose(jnp.broadcast_to(jnp.exp(b_last[n]), (LANES, dk)))[:, 0:1]
        s = decay * s + jnp.dot(ke_t, v, preferred_element_type=F32)
        o_ref[rows, :] = (_rms_rows(o) * out_gate[rows]).astype(o_ref.dtype)
    s_ref[...] = s


def _gla_attention(proj, w_gate, layer, b_gate, norm_g, bsz, dk, dv):
    m = proj.shape[0]
    hk, hv = dk // GLA_HEADS, dv // GLA_HEADS
    c = GLA_STEP
    nt = m // bsz // c
    prefix, masks = _gla_tables(GLA_CHUNK)
    k_off, v_off, r_off, a_off = dk // hk, 2 * dk // hv, (2 * dk + dv) // hv, (2 * dk + 2 * dv) // LANES
    return pl.pallas_call(
        functools.partial(_gla_kernel, scale=hk ** -0.5),
        grid=(bsz, GLA_HEADS, nt),
        in_specs=[
            pl.BlockSpec((c, hk), lambda b, h, t: (b * nt + t, h)),
            pl.BlockSpec((c, hk), lambda b, h, t: (b * nt + t, k_off + h)),
            pl.BlockSpec((c, hv), lambda b, h, t: (b * nt + t, v_off + h)),
            pl.BlockSpec((c, hv), lambda b, h, t: (b * nt + t, r_off + h)),
            pl.BlockSpec((c, LANES), lambda b, h, t: (b * nt + t, a_off)),
            pl.BlockSpec((None, LANES, hk), lambda b, h, t: (layer, 0, h)),
            pl.BlockSpec((1, hk), lambda b, h, t: (0, h)),
            pl.BlockSpec((1, hv), lambda b, h, t: (0, h)),
            pl.BlockSpec(prefix.shape, lambda b, h, t: (0, 0)),
            pl.BlockSpec(masks.shape, lambda b, h, t: (0, 0, 0)),
        ],
        out_specs=pl.BlockSpec((c, hv), lambda b, h, t: (b * nt + t, h)),
        out_shape=jax.ShapeDtypeStruct((m, dv), BF16),
        scratch_shapes=[pltpu.VMEM((hk, hv), F32)],
        compiler_params=_params(("parallel", "parallel", "arbitrary")),
        name="gla_attention",
    )(proj, proj, proj, proj, proj, w_gate, b_gate, norm_g, prefix, masks)


def _mm_post_kernel(a_ref, w_ref, x_ref, pg_ref, gt_ref, o_ref):
    y = jnp.dot(a_ref[...], w_ref[...], preferred_element_type=F32)
    o_ref[...] = _post_norm_residual(x_ref[...], y, pg_ref[...], gt_ref[0])


def _matmul_post(a, w, layer, x, post_g, gate, seq):
    tm = TM_GLA_OUT
    m, kdim = a.shape
    d = w.shape[2]
    tiles_per_batch = seq // tm
    return pl.pallas_call(
        _mm_post_kernel,
        grid=(m // tm,),
        in_specs=[
            pl.BlockSpec((tm, kdim), lambda i: (i, 0)),
            pl.BlockSpec((None, kdim, d), lambda i: (layer, 0, 0)),
            pl.BlockSpec((tm, d), lambda i: (i, 0)),
            pl.BlockSpec((1, d), lambda i: (0, 0)),
            pl.BlockSpec((1, 1, d), lambda i: (i // tiles_per_batch, 0, 0)),
        ],
        out_specs=pl.BlockSpec((tm, d), lambda i: (i, 0)),
        out_shape=jax.ShapeDtypeStruct((m, d), F32),
        compiler_params=_params(("parallel",)),
        name="gla_out_post",
    )(a, w, x, post_g, gate)


def _ffn_kernel(x_ref, g_ref, sc_ref, sh_ref, wg_ref, wu_ref, wo_ref, pg_ref, gt_ref, o_ref, h_ref, act_ref, y_ref,
                *, nf):
    s = pl.program_id(1)
    tf, tn = wg_ref.shape[1], wo_ref.shape[1]
    _prenorm_to_scratch(x_ref, g_ref, sc_ref, sh_ref, h_ref)

    @pl.when(s < nf)
    def _():
        h = h_ref[...]
        gate = jnp.dot(h, wg_ref[...], preferred_element_type=F32)
        up = jnp.dot(h, wu_ref[...], preferred_element_type=F32)
        cols = pl.ds(pl.multiple_of(s * tf, tf), tf)
        act_ref[:, cols] = (gate * jax.nn.sigmoid(gate) * up).astype(BF16)

    @pl.when(s >= nf)
    def _():
        cols = pl.ds(pl.multiple_of((s - nf) * tn, tn), tn)
        y_ref[:, cols] = jnp.dot(act_ref[...], wo_ref[...], preferred_element_type=F32)

    @pl.when(s == pl.num_programs(1) - 1)
    def _():
        _post_norm_rows(y_ref, x_ref, pg_ref, gt_ref, o_ref)


def _ffn(x, gain, scale, shift, w_in, w_out, layer, post_g, gate, seq):
    tm, tf = TM_FFN, TF_FFN
    m, d = x.shape
    dff = w_out.shape[1]
    nf = dff // tf
    tn = min(TN_FFN_OUT, d)
    tiles_per_batch = seq // tm
    return pl.pallas_call(
        functools.partial(_ffn_kernel, nf=nf),
        grid=(m // tm, nf + d // tn),
        in_specs=_row_specs(tm, d, tiles_per_batch) + [
            pl.BlockSpec((None, d, tf), lambda i, s: (layer, 0, jnp.minimum(s, nf - 1))),
            pl.BlockSpec((None, d, tf), lambda i, s: (layer, 0, jnp.minimum(s, nf - 1) + nf)),
            pl.BlockSpec((None, dff, tn), lambda i, s: (layer, 0, jnp.maximum(s - nf, 0))),
            pl.BlockSpec((1, d), lambda i, s: (0, 0)),
            pl.BlockSpec((1, 1, d), lambda i, s: (i // tiles_per_batch, 0, 0)),
        ],
        out_specs=pl.BlockSpec((tm, d), lambda i, s: (i, 0)),
        out_shape=jax.ShapeDtypeStruct((m, d), F32),
        scratch_shapes=[pltpu.VMEM((tm, d), BF16), pltpu.VMEM((tm, dff), BF16), pltpu.VMEM((tm, d), F32)],
        compiler_params=_params(("parallel", "arbitrary")),
        name="ffn",
    )(x, gain, scale, shift, w_in, w_in, w_out, post_g, gate)


def kernel(x, c, w_ada, b_ada, pre_mix_g, post_mix_g, pre_ffn_g, post_ffn_g, conv_w_pw1, conv_b_pw1, conv_w_dw, conv_b_dw, conv_ln_g, conv_ln_b, conv_w_pw2, conv_b_pw2, gla_w_in, gla_w_gate_up, gla_b_gate, gla_norm_g, gla_w_out, ffn_w_in, ffn_w_out):
    bsz, seq, d = x.shape
    depth = w_ada.shape[0]
    dk = gla_w_gate_up.shape[2]
    dv = gla_norm_g.shape[1]
    rank = gla_w_gate_up.shape[1]

    c_pad = jnp.zeros((8, d), F32).at[:bsz].set(c)
    mod = _modulation(c_pad, w_ada, b_ada)[:, :bsz]
    mod = mod.reshape(depth, bsz, N_MOD, 1, d)

    def row(vec):
        return vec.reshape(1, -1)

    pw1_w, pw2_w = conv_w_pw1.astype(BF16), conv_w_pw2.astype(BF16)
    in_width = pl.cdiv(gla_w_in.shape[2], TN_GLA_IN) * TN_GLA_IN
    gla_in_w = jnp.pad(gla_w_in, ((0, 0), (0, 0), (0, in_width - gla_w_in.shape[2]))).astype(BF16)
    gla_gate_w = jnp.pad(gla_w_gate_up, ((0, 0), (0, LANES - rank), (0, 0))).astype(BF16)
    gla_out_w = gla_w_out.astype(BF16)
    ffn_in_w, ffn_out_w = ffn_w_in.astype(BF16), ffn_w_out.astype(BF16)

    xf = x.reshape(bsz * seq, d)
    for i in range(depth):
        sh1, sc1, gt1, sh2, sc2, gt2 = (mod[i, :, n] for n in range(N_MOD))
        j = i // 2
        if i % 2 == 0:
            w_dw = jnp.repeat(conv_w_dw[j], SUBLANES, axis=0)
            cv = _prenorm_glu_conv(xf, row(pre_mix_g[i]), sc1, sh1, pw1_w, j, row(conv_b_pw1[j]),
                                   w_dw, row(conv_b_dw[j]), seq)
            xf = _conv_tail(cv, xf, row(conv_ln_g[j]), row(conv_ln_b[j]),
                            pw2_w, j, row(conv_b_pw2[j]), row(post_mix_g[i]), gt1, seq)
        else:
            proj = _prenorm_matmul(xf, row(pre_mix_g[i]), sc1, sh1, gla_in_w, j, seq)
            o = _gla_attention(proj, gla_gate_w, j, row(gla_b_gate[j]), row(gla_norm_g[j]), bsz, dk, dv)
            xf = _matmul_post(o, gla_out_w, j, xf, row(post_mix_g[i]), gt1, seq)
        xf = _ffn(xf, row(pre_ffn_g[i]), sc2, sh2, ffn_in_w, ffn_out_w, i, row(post_ffn_g[i]), gt2, seq)
    return xf.reshape(bsz, seq, d)
```
